```python
import math
import jax
import jax.numpy as jnp
from jax import lax
import numpy as np

D_MODEL = 1024
BATCH = 2
SEQ = 8192
DEPTH = 2

GRID_W = 64
CTX_LEN = 256
CHUNK = 128

BRANCH_WIDTH = D_MODEL
N_BRANCH = 3
A_WIDTH = BRANCH_WIDTH
A_GROUPS = 4
A_GROUP_DIM = A_WIDTH // A_GROUPS
B_WIDTH = BRANCH_WIDTH
SSM_HEAD_DIM = 64
SSM_HEADS = B_WIDTH // SSM_HEAD_DIM
SSM_GROUPS = 2
SSM_HEADS_PER_GROUP = SSM_HEADS // SSM_GROUPS
SSM_STATE = 128
D_CONV = 5
CONV_DIM = B_WIDTH + 2 * SSM_GROUPS * SSM_STATE
DT_MIN = 0.001
DT_MAX = 0.1
C_WIDTH = BRANCH_WIDTH
HEAD_DIM = 128
N_Q_HEADS = C_WIDTH // HEAD_DIM
N_KV_HEADS = 2
GQA_REP = N_Q_HEADS // N_KV_HEADS
KV_WIDTH = N_KV_HEADS * HEAD_DIM
ROPE_THETA = 10000.0
ROPE_PAIRS_PER_AXIS = HEAD_DIM // 4

IN_SPLITS = (A_WIDTH, A_WIDTH, A_WIDTH,
             B_WIDTH, CONV_DIM, 2 * SSM_HEADS,
             C_WIDTH, KV_WIDTH, KV_WIDTH, C_WIDTH,
             N_BRANCH * D_MODEL)
N_IN = sum(IN_SPLITS)

DEEPNORM_ALPHA = (2 * DEPTH) ** 0.25
DEEPNORM_BETA = (8 * DEPTH) ** -0.25
LN_EPS = 1e-6
RMS_EPS = 1e-6

kernel_name = 'hybrid_sgu_ssd_gqa_diffusion_trunk'


def layer_norm(t):
    tf = t.astype(jnp.float32)
    mu = jnp.mean(tf, -1, keepdims=True)
    var = jnp.mean(jnp.square(tf - mu), -1, keepdims=True)
    return ((tf - mu) * lax.rsqrt(var + LN_EPS)).astype(t.dtype)


def layer_norm_affine(t, g, b):
    return (layer_norm(t).astype(jnp.float32) * g + b).astype(t.dtype)


def rms_norm(t, g):
    tf = t.astype(jnp.float32)
    return (tf * lax.rsqrt(jnp.mean(tf * tf, -1, keepdims=True) + RMS_EPS) * g).astype(t.dtype)


def split_cols(p):
    out, start = [], 0
    for n in IN_SPLITS:
        out.append(p[..., start:start + n])
        start += n
    return out


def heads(t, n_heads):
    return t.reshape(*t.shape[:-1], n_heads, HEAD_DIM)


def chunk_sgu(u, v, w_s, b_s, g_v):
    b, l, _ = v.shape
    nc = l // CHUNK
    vn = layer_norm(v.reshape(b, l, A_GROUPS, A_GROUP_DIM)) * g_v.reshape(A_GROUPS, A_GROUP_DIM)
    vc = vn.reshape(b, nc, CHUNK, A_GROUPS, A_GROUP_DIM)
    mixed = jnp.einsum('gts,bnsgc->bntgc', w_s, vc) + b_s.T[None, None, :, :, None]
    return u * mixed.reshape(b, l, A_WIDTH)


def dwconv_centred(t, w, bias):
    pad = (D_CONV - 1) // 2
    y = lax.conv_general_dilated(t, w[:, None, :], window_strides=(1,), padding=[(pad, pad)],
                                 dimension_numbers=('NWC', 'WIO', 'NWC'),
                                 feature_group_count=t.shape[-1])
    return y + bias


def split_xbc(xbc):
    b, l, _ = xbc.shape
    nb = SSM_GROUPS * SSM_STATE
    xs = xbc[..., :B_WIDTH].reshape(b, l, SSM_GROUPS, SSM_HEADS_PER_GROUP, SSM_HEAD_DIM)
    bm = xbc[..., B_WIDTH:B_WIDTH + nb].reshape(b, l, SSM_GROUPS, SSM_STATE)
    cm = xbc[..., B_WIDTH + nb:].reshape(b, l, SSM_GROUPS, SSM_STATE)
    return xs, bm, cm


def ssd_direction(xs, bm, cm, dt_raw, dt_bias, a_log, d_skip, h0, with_output):
    b, n = xs.shape[:2]
    nc = n // CHUNK
    G, R, P, N = SSM_GROUPS, SSM_HEADS_PER_GROUP, SSM_HEAD_DIM, SSM_STATE
    dt = jax.nn.softplus(dt_raw.astype(jnp.float32) + dt_bias.astype(jnp.float32))
    dt = dt.reshape(b, nc, CHUNK, G, R)
    a_neg = -jnp.exp(a_log.astype(jnp.float32)).reshape(G, R)
    a_cs = jnp.cumsum(dt * a_neg, axis=2)
    xdt = xs.reshape(b, nc, CHUNK, G, R, P) * dt[..., None]
    bc = bm.reshape(b, nc, CHUNK, G, N)
    cc = cm.reshape(b, nc, CHUNK, G, N)
    decay_to_end = jnp.exp(a_cs[:, :, -1:] - a_cs)
    chunk_states = jnp.einsum('bclgn,bclgr,bclgrp->bcgrpn', bc, decay_to_end, xdt)
    chunk_decay = jnp.exp(a_cs[:, :, -1])

    def step(h, inp):
        s_c, d_c = inp
        return h * d_c[..., None, None] + s_c, h

    h_final, h_enter = lax.scan(step, h0, (jnp.moveaxis(chunk_states, 1, 0),
                                           jnp.moveaxis(chunk_decay, 1, 0)))
    if not with_output:
        return h_final
    h_enter = jnp.moveaxis(h_enter, 0, 1)
    mask = jnp.tril(jnp.ones((CHUNK, CHUNK), dtype=bool))
    seg = a_cs[:, :, :, None] - a_cs[:, :, None, :]
    decay_ls = jnp.exp(jnp.where(mask[:, :, None, None], seg, -jnp.inf))
    cb = jnp.einsum('bclgn,bcsgn->bclsg', cc, bc)
    y = (jnp.einsum('bclsg,bclsgr,bcsgrp->bclgrp', cb, decay_ls, xdt)
         + jnp.einsum('bclgn,bcgrpn,bclgr->bclgrp', cc, h_enter, jnp.exp(a_cs)))
    y = y.reshape(b, n, G, R, P) + d_skip.reshape(G, R)[..., None] * xs
    return y.reshape(b, n, B_WIDTH).astype(xs.dtype), h_final


def bidir_ssd(xbc_lat, xbc_ctx, dt_lat, dt_ctx, dt_bias, a_log, d_skip, ctx_out):
    b = xbc_lat.shape[0]
    h0 = jnp.zeros((b, SSM_GROUPS, SSM_HEADS_PER_GROUP, SSM_HEAD_DIM, SSM_STATE), jnp.float32)
    ys_lat, ys_ctx = [], []
    for d in range(2):
        rev = (lambda t: jnp.flip(t, axis=1)) if d == 1 else (lambda t: t)
        prm = (dt_bias[d], a_log[d], d_skip[d])
        lat = [rev(t) for t in split_xbc(xbc_lat)] + [rev(dt_lat[..., d * SSM_HEADS:(d + 1) * SSM_HEADS])]
        ctx = [rev(t) for t in split_xbc(xbc_ctx)] + [rev(dt_ctx[..., d * SSM_HEADS:(d + 1) * SSM_HEADS])]
        if ctx_out:
            y_c, h_c = ssd_direction(*ctx, *prm, h0, True)
            ys_ctx.append(rev(y_c))
        else:
            h_c = ssd_direction(*ctx, *prm, h0, False)
        y_l, _ = ssd_direction(*lat, *prm, h_c, True)
        ys_lat.append(rev(y_l))
    y_ctx = ys_ctx[0] + ys_ctx[1] if ctx_out else None
    return ys_lat[0] + ys_lat[1], y_ctx


def axial_rope_tables(n_tokens):
    rows = n_tokens // GRID_W
    row = jnp.broadcast_to(jnp.arange(rows)[:, None], (rows, GRID_W)).reshape(-1)
    col = jnp.broadcast_to(jnp.arange(GRID_W)[None, :], (rows, GRID_W)).reshape(-1)
    inv = ROPE_THETA ** (-jnp.arange(ROPE_PAIRS_PER_AXIS, dtype=jnp.float32) / ROPE_PAIRS_PER_AXIS)
    ang = jnp.concatenate([row[:, None] * inv, col[:, None] * inv], axis=-1)
    return jnp.cos(ang), jnp.sin(ang)


def apply_rope(t, cos, sin):
    tr = t.astype(jnp.float32).reshape(*t.shape[:-1], HEAD_DIM // 2, 2)
    t0, t1 = tr[..., 0], tr[..., 1]
    cs, sn = cos[None, :, None, :], sin[None, :, None, :]
    out = jnp.stack([t0 * cs - t1 * sn, t0 * sn + t1 * cs], axis=-1)
    return out.reshape(t.shape).astype(t.dtype)


def block_attention(q, k, v):
    b, n = q.shape[:2]
    qb = q.reshape(b, n // CHUNK, CHUNK, N_KV_HEADS, GQA_REP, HEAD_DIM).transpose(1, 0, 2, 3, 4, 5)
    scale = HEAD_DIM ** -0.5

    def one_block(qi):
        s = jnp.einsum('bqkrd,bskd->bkrqs', qi, k).astype(jnp.float32) * scale
        p = jax.nn.softmax(s, axis=-1).astype(v.dtype)
        return jnp.einsum('bkrqs,bskd->bqkrd', p, v)

    o = lax.map(one_block, qb)
    return o.transpose(1, 0, 2, 3, 4, 5).reshape(b, n, C_WIDTH)


def merge_project(y_a, y_b, y_c, g_m, w_branch, w_out, b_out):
    g = jax.nn.sigmoid(g_m.astype(jnp.float32)).astype(y_a.dtype)
    g_a, g_b, g_c = jnp.split(g, N_BRANCH, axis=-1)
    merged = g_a * (y_a @ w_branch[0]) + g_b * (y_b @ w_branch[1]) + g_c * (y_c @ w_branch[2])
    return merged @ w_out + b_out


def trunk_layer(x, xc, c_act, cc_act, cos, sin, w_mod, b_mod, w_in, sgu_w, sgu_b, sgu_norm_g,
                conv_w, conv_b, dt_bias, a_log, d_skip, ssm_norm_g, q_norm_g, k_norm_g,
                w_branch, w_out, b_out, ln_g, ln_b, ctx_out):
    shift, scale, gate = jnp.split((c_act @ w_mod + b_mod)[:, None, :], 3, axis=-1)
    shift_c, scale_c, gate_c = jnp.split(cc_act @ w_mod + b_mod, 3, axis=-1)
    h = layer_norm(x) * (1.0 + scale) + shift
    hc = layer_norm(xc) * (1.0 + scale_c) + shift_c
    u, v, g_a, z, xbc, dt_raw, q, k, vv, g_c, g_m = split_cols(h @ w_in)
    uc, vc, g_ac, zc, xbcc, dt_rawc, qc, kc, vvc, g_cc, g_mc = split_cols(hc @ w_in)

    y_a = chunk_sgu(u, v, sgu_w, sgu_b, sgu_norm_g) * jax.nn.silu(g_a)

    xbc = jax.nn.silu(dwconv_centred(xbc, conv_w, conv_b))
    xbcc = jax.nn.silu(dwconv_centred(xbcc, conv_w, conv_b))
    s_lat, s_ctx = bidir_ssd(xbc, xbcc, dt_raw, dt_rawc, dt_bias, a_log, d_skip, ctx_out)
    y_b = rms_norm(s_lat * jax.nn.silu(z), ssm_norm_g)

    q = apply_rope(rms_norm(heads(q, N_Q_HEADS), q_norm_g), cos, sin)
    k = apply_rope(rms_norm(heads(k, N_KV_HEADS), k_norm_g), cos, sin)
    kc = rms_norm(heads(kc, N_KV_HEADS), k_norm_g)
    vv, vvc = heads(vv, N_KV_HEADS), heads(vvc, N_KV_HEADS)
    k_all = jnp.concatenate([kc, k], axis=1)
    v_all = jnp.concatenate([vvc, vv], axis=1)
    y_c = block_attention(q, k_all, v_all) * jax.nn.silu(g_c)

    out = merge_project(y_a, y_b, y_c, g_m, w_branch, w_out, b_out)
    x_new = layer_norm_affine(DEEPNORM_ALPHA * x + gate * out, ln_g, ln_b)
    if not ctx_out:
        return x_new, None

    y_ac = chunk_sgu(uc, vc, sgu_w, sgu_b, sgu_norm_g) * jax.nn.silu(g_ac)
    y_bc = rms_norm(s_ctx * jax.nn.silu(zc), ssm_norm_g)
    qc = rms_norm(heads(qc, N_Q_HEADS), q_norm_g)
    y_cc = block_attention(qc, kc, vvc) * jax.nn.silu(g_cc)
    out_c = merge_project(y_ac, y_bc, y_cc, g_mc, w_branch, w_out, b_out)
    xc_new = layer_norm_affine(DEEPNORM_ALPHA * xc + gate_c * out_c, ln_g, ln_b)
    return x_new, xc_new


def setup_inputs(seed: int = 0) -> dict:
    key = jax.random.key(seed)
    ks = jax.random.split(key, 23)
    f32 = jnp.float32
    L = DEPTH

    def nrm(k, shape, s):
        return jax.random.normal(k, shape, f32) * s

    dt0 = jnp.exp(jax.random.uniform(ks[12], (L, 2, SSM_HEADS), f32, math.log(DT_MIN), math.log(DT_MAX)))
    return {
        'x': nrm(ks[0], (BATCH, SEQ, D_MODEL), 1.0),
        'c': nrm(ks[1], (BATCH, D_MODEL), 1.0),
        'ctx': nrm(ks[2], (BATCH, CTX_LEN, D_MODEL), 1.0),
        'c_ctx': nrm(ks[3], (D_MODEL,), 1.0),
        'w_mod': nrm(ks[4], (L, D_MODEL, 3 * D_MODEL), 0.5 * D_MODEL ** -0.5),
        'b_mod': nrm(ks[5], (L, 3 * D_MODEL), 0.01),
        'w_in': nrm(ks[6], (L, D_MODEL, N_IN), D_MODEL ** -0.5),
        'sgu_w': nrm(ks[7], (L, A_GROUPS, CHUNK, CHUNK), CHUNK ** -0.5),
        'sgu_b': 1.0 + nrm(ks[8], (L, A_GROUPS, CHUNK), 0.01),
        'sgu_norm_g': 1.0 + nrm(ks[9], (L, A_WIDTH), 0.01),
        'conv_w': nrm(ks[10], (L, D_CONV, CONV_DIM), D_CONV ** -0.5),
        'conv_b': nrm(ks[11], (L, CONV_DIM), 0.01),
        'dt_bias': dt0 + jnp.log(-jnp.expm1(-dt0)),
        'a_log': jnp.log(jax.random.uniform(ks[13], (L, 2, SSM_HEADS), f32, 1.0, 16.0)),
        'd_skip': 1.0 + nrm(ks[14], (L, 2, SSM_HEADS), 0.01),
        'ssm_norm_g': 1.0 + nrm(ks[15], (L, B_WIDTH), 0.01),
        'q_norm_g': 1.0 + nrm(ks[16], (L, HEAD_DIM), 0.01),
        'k_norm_g': 1.0 + nrm(ks[17], (L, HEAD_DIM), 0.01),
        'w_branch': nrm(ks[18], (L, N_BRANCH, BRANCH_WIDTH, D_MODEL), BRANCH_WIDTH ** -0.5 * DEEPNORM_BETA),
        'w_out': nrm(ks[19], (L, D_MODEL, D_MODEL), D_MODEL ** -0.5 * DEEPNORM_BETA),
        'b_out': nrm(ks[20], (L, D_MODEL), 0.01),
        'ln_g': 1.0 + nrm(ks[21], (L, D_MODEL), 0.01),
        'ln_b': nrm(ks[22], (L, D_MODEL), 0.01),
    }


def reference(x, c, ctx, c_ctx, w_mod, b_mod, w_in, sgu_w, sgu_b, sgu_norm_g, conv_w, conv_b,
              dt_bias, a_log, d_skip, ssm_norm_g, q_norm_g, k_norm_g, w_branch, w_out, b_out,
              ln_g, ln_b):
    cos, sin = axial_rope_tables(x.shape[1])
    c_act = jax.nn.silu(c)
    cc_act = jax.nn.silu(c_ctx)
    xc = ctx
    for l in range(DEPTH):
        x, xc = trunk_layer(x, xc, c_act, cc_act, cos, sin, w_mod[l], b_mod[l], w_in[l],
                            sgu_w[l], sgu_b[l], sgu_norm_g[l], conv_w[l], conv_b[l],
                            dt_bias[l], a_log[l], d_skip[l], ssm_norm_g[l], q_norm_g[l],
                            k_norm_g[l], w_branch[l], w_out[l], b_out[l], ln_g[l], ln_b[l],
                            ctx_out=(l < DEPTH - 1))
    return x
```

```python
import functools
import math

import jax
import jax.numpy as jnp
from jax import lax
from jax.experimental import pallas as pl
from jax.experimental.pallas import tpu as pltpu

D_MODEL = 1024
DEPTH = 2
GRID_W = 64
CHUNK = 128
A_GROUPS = 4
A_GROUP_DIM = D_MODEL // A_GROUPS
SSM_HEAD_DIM = 64
SSM_HEADS = 16
SSM_GROUPS = 2
SSM_STATE = 128
D_CONV = 5
CONV_DIM = D_MODEL + 2 * SSM_GROUPS * SSM_STATE
HEAD_DIM = 128
N_Q_HEADS = 8
N_KV_HEADS = 2
GQA_REP = N_Q_HEADS // N_KV_HEADS
KV_WIDTH = N_KV_HEADS * HEAD_DIM
ROPE_THETA = 10000.0
ROPE_PAIRS_PER_AXIS = HEAD_DIM // 4
DEEPNORM_ALPHA = (2 * DEPTH) ** 0.25
LN_EPS = 1e-6
RMS_EPS = 1e-6

OFF_U, OFF_V, OFF_GA, OFF_Z, OFF_Q, OFF_GC, OFF_GM = 0, 1024, 2048, 3072, 4096, 5120, 6144
OFF_XBC = 9216
OFF_K = OFF_XBC + CONV_DIM
OFF_VV = OFF_K + KV_WIDTH
OFF_DT = OFF_VV + KV_WIDTH
DT_PAD = 256
N_PACK = OFF_DT + DT_PAD
PROJ_TN = 1280

VMEM_LIMIT = 56 * 1024 * 1024
F32 = jnp.float32
BF16 = jnp.bfloat16
NEG_BIG = -1e30


def _cparams(sem):
    return pltpu.CompilerParams(dimension_semantics=sem, vmem_limit_bytes=VMEM_LIMIT)


def _silu(t):
    return t * jax.nn.sigmoid(t)


def _split_dot(x, e):
    x1 = x.astype(BF16)
    r1 = x - x1.astype(F32)
    x2 = r1.astype(BF16)
    r2 = r1 - x2.astype(F32)
    x3 = r2.astype(BF16)
    acc = jnp.dot(x1, e, preferred_element_type=F32)
    acc = acc + jnp.dot(x2, e, preferred_element_type=F32)
    return acc + jnp.dot(x3, e, preferred_element_type=F32)


def _split_dot_rhs(e, x):
    x1 = x.astype(BF16)
    r1 = x - x1.astype(F32)
    x2 = r1.astype(BF16)
    r2 = r1 - x2.astype(F32)
    x3 = r2.astype(BF16)
    acc = jnp.dot(e, x1, preferred_element_type=F32)
    acc = acc + jnp.dot(e, x2, preferred_element_type=F32)
    return acc + jnp.dot(e, x3, preferred_element_type=F32)


def _mod_kernel(c_ref, w_ref, b_ref, o_ref):
    c = c_ref[...]
    act = _silu(c)
    o_ref[...] = jnp.dot(act, w_ref[...], preferred_element_type=F32,
                         precision=lax.Precision.HIGHEST) + b_ref[...]


def _modulation(c_rows, w_mod, b_mod):
    L = w_mod.shape[0]
    tn = 512
    return pl.pallas_call(
        _mod_kernel,
        out_shape=jax.ShapeDtypeStruct((L, 8, 3 * D_MODEL), F32),
        grid=(L, 3 * D_MODEL // tn),
        in_specs=[pl.BlockSpec((8, D_MODEL), lambda l, j: (0, 0)),
                  pl.BlockSpec((None, D_MODEL, tn), lambda l, j: (l, 0, j)),
                  pl.BlockSpec((None, 1, tn), lambda l, j: (l, 0, j))],
        out_specs=pl.BlockSpec((None, 8, tn), lambda l, j: (l, 0, j)),
        compiler_params=_cparams(("arbitrary", "arbitrary")),
        name="adaln_modulation",
    )(c_rows, w_mod, b_mod)


def _inproj_kernel(x_ref, mod_ref, w_ref, o_ref, h_ref):
    @pl.when(pl.program_id(1) == 0)
    def _():
        x = x_ref[...]
        mu = jnp.mean(x, axis=-1, keepdims=True)
        xc = x - mu
        var = jnp.mean(xc * xc, axis=-1, keepdims=True)
        shift = mod_ref[:, 0:D_MODEL]
        scale = mod_ref[:, D_MODEL:2 * D_MODEL]
        h = xc * lax.rsqrt(var + LN_EPS) * (1.0 + scale) + shift
        h_ref[...] = h.astype(BF16)

    o_ref[...] = jnp.dot(h_ref[...], w_ref[...], preferred_element_type=F32)


def _inproj(x2d, mod, w_pack, tm, mod_row):
    T = x2d.shape[0]
    return pl.pallas_call(
        _inproj_kernel,
        out_shape=jax.ShapeDtypeStruct((T, N_PACK), F32),
        grid=(T // tm, N_PACK // PROJ_TN),
        in_specs=[pl.BlockSpec((tm, D_MODEL), lambda i, j: (i, 0)),
                  pl.BlockSpec((None, 1, 3 * D_MODEL), lambda i, j: (mod_row(i), 0, 0)),
                  pl.BlockSpec((D_MODEL, PROJ_TN), lambda i, j: (0, j))],
        out_specs=pl.BlockSpec((tm, PROJ_TN), lambda i, j: (i, j)),
        scratch_shapes=[pltpu.VMEM((tm, D_MODEL), BF16)],
        compiler_params=_cparams(("arbitrary", "arbitrary")),
        name="ln_mod_inproj",
    )(x2d, mod, w_pack)


def _sgu_kernel(u_ref, v_ref, ga_ref, w_ref, b_ref, g_ref, o_ref, *, n_chunks):
    for c in range(n_chunks):
        rows = slice(c * CHUNK, (c + 1) * CHUNK)
        for g in range(A_GROUPS):
            cols = slice(g * A_GROUP_DIM, (g + 1) * A_GROUP_DIM)
            v = v_ref[rows, cols]
            mu = jnp.mean(v, axis=-1, keepdims=True)
            vc = v - mu
            var = jnp.mean(vc * vc, axis=-1, keepdims=True)
            vn = vc * lax.rsqrt(var + LN_EPS) * g_ref[:, cols]
            mixed = jnp.dot(w_ref[g], vn.astype(BF16), preferred_element_type=F32)
            mixed = mixed + b_ref[:, cols]
            y = u_ref[rows, cols] * mixed * _silu(ga_ref[rows, cols])
            o_ref[rows, cols] = y.astype(o_ref.dtype)


def _sgu(p, sgu_w, sgu_bfull, sgu_g, tm):
    T = p.shape[0]
    blk = lambda j: pl.BlockSpec((tm, D_MODEL), lambda i, j=j: (i, j))
    return pl.pallas_call(
        functools.partial(_sgu_kernel, n_chunks=tm // CHUNK),
        out_shape=jax.ShapeDtypeStruct((T, D_MODEL), BF16),
        grid=(T // tm,),
        in_specs=[blk(OFF_U // D_MODEL), blk(OFF_V // D_MODEL), blk(OFF_GA // D_MODEL),
                  pl.BlockSpec((A_GROUPS, CHUNK, CHUNK), lambda i: (0, 0, 0)),
                  pl.BlockSpec((CHUNK, D_MODEL), lambda i: (0, 0)),
                  pl.BlockSpec((1, D_MODEL), lambda i: (0, 0))],
        out_specs=pl.BlockSpec((tm, D_MODEL), lambda i: (i, 0)),
        compiler_params=_cparams(("arbitrary",)),
        name="chunk_sgu",
    )(p, p, p, sgu_w, sgu_bfull, sgu_g)


CONV_HALO = 8


def _conv_kernel(prev_ref, cur_ref, next_ref, w_ref, b_ref, o_ref, ext_ref, *, tiles_per_seq, tc):
    i = pl.program_id(0)
    pos = i % tiles_per_seq
    zeros = jnp.zeros((CONV_HALO, CONV_DIM), F32)
    ext_ref[0:CONV_HALO, :] = jnp.where(pos == 0, zeros, prev_ref[...])
    ext_ref[CONV_HALO:CONV_HALO + tc, :] = cur_ref[...]
    ext_ref[CONV_HALO + tc:, :] = jnp.where(pos == tiles_per_seq - 1, zeros, next_ref[...])
    pad = (D_CONV - 1) // 2
    acc = jnp.zeros((tc, CONV_DIM), F32) + b_ref[...]
    for j in range(D_CONV):
        start = CONV_HALO - pad + j
        acc = acc + ext_ref[start:start + tc, :] * w_ref[j:j + 1, :]
    o_ref[...] = _silu(acc)


def _conv(p, conv_w, conv_b, seq_len, tc):
    T = p.shape[0]
    n_tiles = T // tc
    hb = tc // CONV_HALO
    n_hblocks = T // CONV_HALO
    colb = OFF_XBC // CONV_DIM
    return pl.pallas_call(
        functools.partial(_conv_kernel, tiles_per_seq=seq_len // tc, tc=tc),
        out_shape=jax.ShapeDtypeStruct((T, CONV_DIM), F32),
        grid=(n_tiles,),
        in_specs=[pl.BlockSpec((CONV_HALO, CONV_DIM), lambda i: (jnp.maximum(i * hb - 1, 0), colb)),
                  pl.BlockSpec((tc, CONV_DIM), lambda i: (i, colb)),
                  pl.BlockSpec((CONV_HALO, CONV_DIM),
                               lambda i: (jnp.minimum((i + 1) * hb, n_hblocks - 1), colb)),
                  pl.BlockSpec((D_CONV, CONV_DIM), lambda i: (0, 0)),
                  pl.BlockSpec((1, CONV_DIM), lambda i: (0, 0))],
        out_specs=pl.BlockSpec((tc, CONV_DIM), lambda i: (i, 0)),
        scratch_shapes=[pltpu.VMEM((tc + 2 * CONV_HALO, CONV_DIM), F32)],
        compiler_params=_cparams(("arbitrary",)),
        name="dwconv_silu",
    )(p, p, p, conv_w, conv_b)


N_CTX_CHUNKS = 2


def _ssd_kernel(xc_ref, xl_ref, dtc_ref, dtl_ref, dtb_ref, alog_ref, dskip_ref, e_ref,
                yc_ref, yl_ref, ht_ref, *, d, reverse):
    s = pl.program_id(1)
    is_ctx = s < N_CTX_CHUNKS

    @pl.when(s == 0)
    def _():
        ht_ref[...] = jnp.zeros_like(ht_ref)

    x = jnp.where(is_ctx, xc_ref[...], xl_ref[...])
    dt_raw = jnp.where(is_ctx, dtc_ref[:, 0:CHUNK], dtl_ref[:, 0:CHUNK])
    dt = jax.nn.softplus(dt_raw + dtb_ref[...])
    a = dt * (-jnp.exp(alog_ref[...]))

    row = lax.broadcasted_iota(jnp.int32, (CHUNK, CHUNK), 0)
    col = lax.broadcasted_iota(jnp.int32, (CHUNK, CHUNK), 1)
    mask = (row <= col) if reverse else (row >= col)
    tri = mask.astype(BF16)
    tri_t = ((row >= col) if reverse else (row <= col)).astype(BF16)

    a_cs = _split_dot_rhs(tri, a)
    a_cs_t = _split_dot(a.T, tri_t)

    e = e_ref[...]
    a_cs_x = _split_dot(a_cs, e)
    dt_x = _split_dot(dt, e)
    end = 0 if reverse else CHUNK - 1
    a_tot_x = a_cs_x[end:end + 1, :]

    xs = x[:, 0:D_MODEL]
    xdt = xs * dt_x
    xdt_bf = xdt.astype(BF16)
    xw = (xdt * jnp.exp(a_tot_x - a_cs_x)).astype(BF16)
    exp_acs_x = jnp.exp(a_cs_x)
    chunk_decay_x = jnp.exp(a_tot_x)

    lane = lax.broadcasted_iota(jnp.int32, (CHUNK, 2 * SSM_HEAD_DIM), 1)
    left = lane < SSM_HEAD_DIM
    gw = D_MODEL // SSM_GROUPS
    heads_per_group = SSM_HEADS // SSM_GROUPS
    y_parts = []
    for g in range(SSM_GROUPS):
        b_g = x[:, D_MODEL + g * SSM_STATE:D_MODEL + (g + 1) * SSM_STATE]
        c_g = x[:, D_MODEL + SSM_GROUPS * SSM_STATE + g * SSM_STATE:
                D_MODEL + SSM_GROUPS * SSM_STATE + (g + 1) * SSM_STATE]
        b_bf = b_g.astype(BF16)
        c_bf = c_g.astype(BF16)
        cb = lax.dot_general(c_bf, b_bf, (((1,), (1,)), ((), ())), preferred_element_type=F32)
        gcols = slice(g * gw, (g + 1) * gw)
        h_in = ht_ref[g]
        y_inter = jnp.dot(c_bf, h_in.astype(BF16), preferred_element_type=F32) * exp_acs_x[:, gcols]
        b_t = b_g.T.astype(BF16)
        ht_ref[g] = h_in * chunk_decay_x[:, gcols] + jnp.dot(b_t, xw[:, gcols], preferred_element_type=F32)
        for j in range(heads_per_group // 2):
            ms = []
            for k in range(2):
                hh = d * SSM_HEADS + g * heads_per_group + 2 * j + k
                seg = a_cs[:, hh:hh + 1] - a_cs_t[hh:hh + 1, :]
                dec = jnp.exp(jnp.where(mask, seg, NEG_BIG))
                ms.append((cb * dec).astype(BF16))
            lhs = jnp.concatenate(ms, axis=1)
            c0 = (g * heads_per_group + 2 * j) * SSM_HEAD_DIM
            xp = xdt_bf[:, c0:c0 + 2 * SSM_HEAD_DIM]
            zero = jnp.zeros_like(xp)
            rhs = jnp.concatenate([jnp.where(left, xp, zero), jnp.where(left, zero, xp)], axis=0)
            y_pair = jnp.dot(lhs, rhs, preferred_element_type=F32)
            y_parts.append(y_pair + y_inter[:, 2 * j * SSM_HEAD_DIM:(2 * j + 2) * SSM_HEAD_DIM])
    y = jnp.concatenate(y_parts, axis=1) + dskip_ref[...] * xs

    @pl.when(is_ctx)
    def _():
        yc_ref[...] = y

    @pl.when(jnp.logical_not(is_ctx))
    def _():
        yl_ref[...] = y


def _ssd(xconv_ctx, xconv_lat, p_ctx, p_lat, dtb, alog, dskip_x, e_mat, batch, d):
    reverse = d == 1
    n_lat = xconv_lat.shape[0] // batch // CHUNK
    n_steps = N_CTX_CHUNKS + n_lat
    dt_colb = OFF_DT // DT_PAD

    if reverse:
        cidx = lambda s: jnp.maximum(N_CTX_CHUNKS - 1 - s, 0)
        lidx = lambda s: jnp.minimum(n_steps - 1 - s, n_lat - 1)
    else:
        cidx = lambda s: jnp.minimum(s, N_CTX_CHUNKS - 1)
        lidx = lambda s: jnp.maximum(s - N_CTX_CHUNKS, 0)

    crow = lambda b, s: b * N_CTX_CHUNKS + cidx(s)
    lrow = lambda b, s: b * n_lat + lidx(s)
    const = lambda shape: pl.BlockSpec(shape, lambda b, s: (0,) * len(shape))
    return pl.pallas_call(
        functools.partial(_ssd_kernel, d=d, reverse=reverse),
        out_shape=(jax.ShapeDtypeStruct((xconv_ctx.shape[0], D_MODEL), F32),
                   jax.ShapeDtypeStruct((xconv_lat.shape[0], D_MODEL), F32)),
        grid=(batch, n_steps),
        in_specs=[pl.BlockSpec((CHUNK, CONV_DIM), lambda b, s: (crow(b, s), 0)),
                  pl.BlockSpec((CHUNK, CONV_DIM), lambda b, s: (lrow(b, s), 0)),
                  pl.BlockSpec((CHUNK, DT_PAD), lambda b, s: (crow(b, s), dt_colb)),
                  pl.BlockSpec((CHUNK, DT_PAD), lambda b, s: (lrow(b, s), dt_colb)),
                  const((1, CHUNK)), const((1, CHUNK)), const((1, D_MODEL)),
                  const((CHUNK, D_MODEL))],
        out_specs=(pl.BlockSpec((CHUNK, D_MODEL), lambda b, s: (crow(b, s), 0)),
                   pl.BlockSpec((CHUNK, D_MODEL), lambda b, s: (lrow(b, s), 0))),
        scratch_shapes=[pltpu.VMEM((SSM_GROUPS, SSM_STATE, D_MODEL // SSM_GROUPS), F32)],
        compiler_params=_cparams(("arbitrary", "arbitrary")),
        name="ssd_scan_rev" if reverse else "ssd_scan_fwd",
    )(xconv_ctx, xconv_lat, p_ctx, p_lat, dtb, alog, dskip_x, e_mat)


def _norm_rope(t, g, cos, sin):
    tn = t * lax.rsqrt(jnp.mean(t * t, axis=-1, keepdims=True) + RMS_EPS) * g
    return tn * cos + pltpu.roll(tn, HEAD_DIM // 2, axis=1) * sin


def _kvprep_kernel(k_ref, v_ref, cos_ref, sin_ref, g_ref, ko_ref, vto_ref):
    for h in range(N_KV_HEADS):
        cols = slice(h * HEAD_DIM, (h + 1) * HEAD_DIM)
        kh = _norm_rope(k_ref[:, cols], g_ref[...], cos_ref[...], sin_ref[...])
        ko_ref[h] = kh.astype(BF16)
        vto_ref[h] = v_ref[:, cols].T.astype(BF16)


def _kvprep(p, cos, sin, k_g, batch, tm):
    T = p.shape[0]
    S = T // batch
    nt = S // tm
    return pl.pallas_call(
        _kvprep_kernel,
        out_shape=(jax.ShapeDtypeStruct((batch, N_KV_HEADS, S, HEAD_DIM), BF16),
                   jax.ShapeDtypeStruct((batch, N_KV_HEADS, HEAD_DIM, S), BF16)),
        grid=(batch, nt),
        in_specs=[pl.BlockSpec((tm, KV_WIDTH), lambda b, i: (b * nt + i, OFF_K // KV_WIDTH)),
                  pl.BlockSpec((tm, KV_WIDTH), lambda b, i: (b * nt + i, OFF_VV // KV_WIDTH)),
                  pl.BlockSpec((tm, HEAD_DIM), lambda b, i: (i, 0)),
                  pl.BlockSpec((tm, HEAD_DIM), lambda b, i: (i, 0)),
                  pl.BlockSpec((1, HEAD_DIM), lambda b, i: (0, 0))],
        out_specs=(pl.BlockSpec((None, N_KV_HEADS, tm, HEAD_DIM), lambda b, i: (b, 0, i, 0)),
                   pl.BlockSpec((None, N_KV_HEADS, HEAD_DIM, tm), lambda b, i: (b, 0, 0, i))),
        compiler_params=_cparams(("arbitrary", "arbitrary")),
        name="kv_prep",
    )(p, p, cos, sin, k_g)


def _attn_kernel(*refs, tq, tk, n_lat_chunks):
    if n_lat_chunks:
        q_ref, cos_ref, sin_ref, g_ref, kc_ref, vtc_ref, kl_ref, vtl_ref, o_ref = refs
    else:
        q_ref, cos_ref, sin_ref, g_ref, kc_ref, vtc_ref, o_ref = refs
    scale = HEAD_DIM ** -0.5
    nt = (((1,), (1,)), ((), ()))
    for h in range(GQA_REP):
        cols = slice(h * HEAD_DIM, (h + 1) * HEAD_DIM)
        qh = _norm_rope(q_ref[:, cols], g_ref[...], cos_ref[...], sin_ref[...]) * scale
        qh = qh.astype(BF16)
        st = lax.dot_general(kc_ref[...], qh, nt, preferred_element_type=F32)
        m = jnp.max(st, axis=0, keepdims=True)
        p = jnp.exp(st - m)
        l = jnp.sum(p, axis=0, keepdims=True)
        acc = jnp.dot(vtc_ref[...], p.astype(BF16), preferred_element_type=F32)

        if n_lat_chunks:
            def body(j, carry):
                m, l, acc = carry
                off = pl.multiple_of(j * tk, tk)
                st = lax.dot_general(kl_ref[pl.ds(off, tk), :], qh, nt, preferred_element_type=F32)
                m_new = jnp.maximum(m, jnp.max(st, axis=0, keepdims=True))
                alpha = jnp.exp(m - m_new)
                p = jnp.exp(st - m_new)
                l = alpha * l + jnp.sum(p, axis=0, keepdims=True)
                acc = alpha * acc + jnp.dot(vtl_ref[:, pl.ds(off, tk)], p.astype(BF16),
                                            preferred_element_type=F32)
                return m_new, l, acc

            m, l, acc = lax.fori_loop(0, n_lat_chunks, body, (m, l, acc))
        o_ref[:, cols] = (acc / l).T


def _attention(p_q, cos, sin, q_g, k_ctx, vt_ctx, k_lat, vt_lat, batch, tq, tk):
    T = p_q.shape[0]
    S = T // batch
    nq = S // tq
    n_ctx = k_ctx.shape[2]
    n_lat_chunks = 0 if k_lat is None else k_lat.shape[2] // tk
    gw = GQA_REP * HEAD_DIM
    in_specs = [pl.BlockSpec((tq, gw), lambda b, i, g: (b * nq + i, OFF_Q // gw + g)),
                pl.BlockSpec((tq, HEAD_DIM), lambda b, i, g: (i, 0)),
                pl.BlockSpec((tq, HEAD_DIM), lambda b, i, g: (i, 0)),
                pl.BlockSpec((1, HEAD_DIM), lambda b, i, g: (0, 0)),
                pl.BlockSpec((None, None, n_ctx, HEAD_DIM), lambda b, i, g: (b, g, 0, 0)),
                pl.BlockSpec((None, None, HEAD_DIM, n_ctx), lambda b, i, g: (b, g, 0, 0))]
    args = [p_q, cos, sin, q_g, k_ctx, vt_ctx]
    if n_lat_chunks:
        n_lat = k_lat.shape[2]
        in_specs += [pl.BlockSpec((None, None, n_lat, HEAD_DIM), lambda b, i, g: (b, g, 0, 0)),
                     pl.BlockSpec((None, None, HEAD_DIM, n_lat), lambda b, i, g: (b, g, 0, 0))]
        args += [k_lat, vt_lat]
    return pl.pallas_call(
        functools.partial(_attn_kernel, tq=tq, tk=tk, n_lat_chunks=n_lat_chunks),
        out_shape=jax.ShapeDtypeStruct((T, D_MODEL), F32),
        grid=(batch, nq, N_KV_HEADS),
        in_specs=in_specs,
        out_specs=pl.BlockSpec((tq, gw), lambda b, i, g: (b * nq + i, g)),
        compiler_params=_cparams(("arbitrary", "arbitrary", "arbitrary")),
        name="gqa_flash_lat" if n_lat_chunks else "gqa_flash_ctx",
    )(*args)


def _merge_kernel(ya_ref, sf_ref, sb_ref, z_ref, o_ref, gc_ref, gm_ref, x_ref, mod_ref,
                  wb_ref, wo_ref, bo_ref, sg_ref, lg_ref, lb_ref, out_ref):
    s = (sf_ref[...] + sb_ref[...]) * _silu(z_ref[...])
    yb = s * lax.rsqrt(jnp.mean(s * s, axis=-1, keepdims=True) + RMS_EPS) * sg_ref[...]
    yc = o_ref[...] * _silu(gc_ref[...])
    merged = jax.nn.sigmoid(gm_ref[:, 0:D_MODEL]) * jnp.dot(
        ya_ref[...], wb_ref[0], preferred_element_type=F32)
    merged = merged + jax.nn.sigmoid(gm_ref[:, D_MODEL:2 * D_MODEL]) * jnp.dot(
        yb.astype(BF16), wb_ref[1], preferred_element_type=F32)
    merged = merged + jax.nn.sigmoid(gm_ref[:, 2 * D_MODEL:3 * D_MODEL]) * jnp.dot(
        yc.astype(BF16), wb_ref[2], preferred_element_type=F32)
    out = jnp.dot(merged.astype(BF16), wo_ref[...], preferred_element_type=F32) + bo_ref[...]
    gate = mod_ref[:, 2 * D_MODEL:3 * D_MODEL]
    r = DEEPNORM_ALPHA * x_ref[...] + gate * out
    mu = jnp.mean(r, axis=-1, keepdims=True)
    rc = r - mu
    var = jnp.mean(rc * rc, axis=-1, keepdims=True)
    out_ref[...] = rc * lax.rsqrt(var + LN_EPS) * lg_ref[...] + lb_ref[...]


def _merge(y_a, s_f, s_b, p, o, x2d, mod, w_branch, w_out, b_out, ssm_g, ln_g, ln_b, tm, mod_row):
    T = x2d.shape[0]
    row = lambda: pl.BlockSpec((tm, D_MODEL), lambda i: (i, 0))
    pcol = lambda off: pl.BlockSpec((tm, D_MODEL), lambda i: (i, off // D_MODEL))
    vec = lambda: pl.BlockSpec((1, D_MODEL), lambda i: (0, 0))
    return pl.pallas_call(
        _merge_kernel,
        out_shape=jax.ShapeDtypeStruct((T, D_MODEL), F32),
        grid=(T // tm,),
        in_specs=[row(), row(), row(), pcol(OFF_Z), row(), pcol(OFF_GC),
                  pl.BlockSpec((tm, 3 * D_MODEL), lambda i: (i, OFF_GM // (3 * D_MODEL))),
                  row(),
                  pl.BlockSpec((None, 1, 3 * D_MODEL), lambda i: (mod_row(i), 0, 0)),
                  pl.BlockSpec((3, D_MODEL, D_MODEL), lambda i: (0, 0, 0)),
                  pl.BlockSpec((D_MODEL, D_MODEL), lambda i: (0, 0)),
                  vec(), vec(), vec(), vec()],
        out_specs=row(),
        compiler_params=_cparams(("arbitrary",)),
        name="merge_out_deepnorm",
    )(y_a, s_f, s_b, p, o, p, p, x2d, mod, w_branch, w_out, b_out, ssm_g, ln_g, ln_b)


def _deinterleave_cols(w, n_heads):
    r = w.shape[0]
    return w.reshape(r, n_heads, HEAD_DIM // 2, 2).transpose(0, 1, 3, 2).reshape(r, n_heads * HEAD_DIM)


def _pack_w_in(w):
    o = 0
    seg = {}
    for name, n in (("u", 1024), ("v", 1024), ("ga", 1024), ("z", 1024), ("xbc", CONV_DIM),
                    ("dt", 2 * SSM_HEADS), ("q", 1024), ("k", KV_WIDTH), ("vv", KV_WIDTH),
                    ("gc", 1024), ("gm", 3 * D_MODEL)):
        seg[name] = w[:, o:o + n]
        o += n
    pad = jnp.zeros((w.shape[0], DT_PAD - 2 * SSM_HEADS), w.dtype)
    packed = jnp.concatenate(
        [seg["u"], seg["v"], seg["ga"], seg["z"], _deinterleave_cols(seg["q"], N_Q_HEADS),
         seg["gc"], seg["gm"], seg["xbc"], _deinterleave_cols(seg["k"], N_KV_HEADS), seg["vv"],
         seg["dt"], pad], axis=1)
    return packed.astype(BF16)


def _rope_tables(n_tokens):
    rows = n_tokens // GRID_W
    row = jnp.broadcast_to(jnp.arange(rows)[:, None], (rows, GRID_W)).reshape(-1)
    col = jnp.broadcast_to(jnp.arange(GRID_W)[None, :], (rows, GRID_W)).reshape(-1)
    inv = ROPE_THETA ** (-jnp.arange(ROPE_PAIRS_PER_AXIS, dtype=F32) / ROPE_PAIRS_PER_AXIS)
    ang = jnp.concatenate([row[:, None] * inv, col[:, None] * inv], axis=-1)
    cos, sin = jnp.cos(ang), jnp.sin(ang)
    return jnp.concatenate([cos, cos], axis=-1), jnp.concatenate([-sin, sin], axis=-1)


def _pad_lanes(v, n):
    return jnp.concatenate([v, jnp.zeros((n - v.shape[0],), v.dtype)])[None, :]


def kernel(x, c, ctx, c_ctx, w_mod, b_mod, w_in, sgu_w, sgu_b, sgu_norm_g, conv_w, conv_b, dt_bias,
           a_log, d_skip, ssm_norm_g, q_norm_g, k_norm_g, w_branch, w_out, b_out, ln_g, ln_b):
    batch, seq, _ = x.shape
    n_ctx = ctx.shape[1]
    assert n_ctx == N_CTX_CHUNKS * CHUNK and seq % 1024 == 0

    cos_l, sin_l = _rope_tables(seq)
    cos_c = jnp.ones((n_ctx, HEAD_DIM), F32)
    sin_c = jnp.zeros((n_ctx, HEAD_DIM), F32)

    c_rows = jnp.concatenate([c, c_ctx[None, :], jnp.zeros((8 - batch - 1, D_MODEL), F32)], axis=0)
    mod_all = _modulation(c_rows, w_mod, b_mod[:, None, :])
    ctx_row = batch

    hid = jnp.arange(CHUNK)[:, None]
    ch = jnp.arange(D_MODEL)[None, :] // SSM_HEAD_DIM
    e_mats = [(hid == ch + d * SSM_HEADS).astype(BF16) for d in range(2)]

    lat_per_tile = 1024
    tiles_per_batch = seq // lat_per_tile
    xl = x.reshape(batch * seq, D_MODEL)
    xc = ctx.reshape(batch * n_ctx, D_MODEL)

    for l in range(DEPTH):
        ctx_out = l < DEPTH - 1
        mod = mod_all[l][:, None, :]
        w_pack = _pack_w_in(w_in[l])
        wb = w_branch[l].astype(BF16)
        wo = w_out[l].astype(BF16)
        perm_g = lambda g: g.reshape(HEAD_DIM // 2, 2).T.reshape(1, HEAD_DIM)
        q_g, k_g = perm_g(q_norm_g[l]), perm_g(k_norm_g[l])
        sgu_wb = sgu_w[l].astype(BF16)
        sgu_bfull = jnp.repeat(sgu_b[l].T, A_GROUP_DIM, axis=1)
        sgu_g = sgu_norm_g[l][None, :]
        dtb = _pad_lanes(dt_bias[l].reshape(-1), CHUNK)
        alog = _pad_lanes(a_log[l].reshape(-1), CHUNK)

        lat_row = lambda i: i // tiles_per_batch
        p_lat = _inproj(xl, mod, w_pack, lat_per_tile, lat_row)
        p_ctx = _inproj(xc, mod, w_pack, n_ctx, lambda i: ctx_row)

        xconv_lat = _conv(p_lat, conv_w[l], conv_b[l][None, :], seq, 512)
        xconv_ctx = _conv(p_ctx, conv_w[l], conv_b[l][None, :], n_ctx, n_ctx)
        ssd = []
        for d in range(2):
            dskip_x = jnp.repeat(d_skip[l, d], SSM_HEAD_DIM)[None, :]
            ssd.append(_ssd(xconv_ctx, xconv_lat, p_ctx, p_lat, dtb, alog, dskip_x, e_mats[d], batch, d))

        k_ctx, vt_ctx = _kvprep(p_ctx, cos_c, sin_c, k_g, batch, n_ctx)
        k_lat, vt_lat = _kvprep(p_lat, cos_l, sin_l, k_g, batch, 512)
        o_lat = _attention(p_lat, cos_l, sin_l, q_g, k_ctx, vt_ctx, k_lat, vt_lat, batch, 256, 512)

        ya_lat = _sgu(p_lat, sgu_wb, sgu_bfull, sgu_g, 512)

        vecs = (b_out[l][None, :], ssm_norm_g[l][None, :], ln_g[l][None, :], ln_b[l][None, :])
        merge_tiles_per_batch = seq // 256
        xl_new = _merge(ya_lat, ssd[0][1], ssd[1][1], p_lat, o_lat, xl, mod, wb, wo, *vecs,
                        256, lambda i: i // merge_tiles_per_batch)
        if ctx_out:
            o_ctx = _attention(p_ctx, cos_c, sin_c, q_g, k_ctx, vt_ctx, None, None, batch, n_ctx, 0)
            ya_ctx = _sgu(p_ctx, sgu_wb, sgu_bfull, sgu_g, n_ctx)
            xc = _merge(ya_ctx, ssd[0][0], ssd[1][0], p_ctx, o_ctx, xc, mod, wb, wo, *vecs,
                        n_ctx, lambda i: ctx_row)
        xl = xl_new

    return xl.reshape(batch, seq, D_MODEL)
```

```python
import functools
import math

import jax
import jax.numpy as jnp
from jax import lax
from jax.experimental import pallas as pl
from jax.experimental.pallas import tpu as pltpu

D_MODEL = 1024
DEPTH = 2
GRID_W = 64
CHUNK = 128
A_GROUPS = 4
A_GROUP_DIM = D_MODEL // A_GROUPS
SSM_HEAD_DIM = 64
SSM_HEADS = 16
SSM_GROUPS = 2
SSM_STATE = 128
D_CONV = 5
CONV_DIM = D_MODEL + 2 * SSM_GROUPS * SSM_STATE
HEAD_DIM = 128
N_Q_HEADS = 8
N_KV_HEADS = 2
GQA_REP = N_Q_HEADS // N_KV_HEADS
KV_WIDTH = N_KV_HEADS * HEAD_DIM
ROPE_THETA = 10000.0
ROPE_PAIRS_PER_AXIS = HEAD_DIM // 4
DEEPNORM_ALPHA = (2 * DEPTH) ** 0.25
LN_EPS = 1e-6
RMS_EPS = 1e-6

OFF_U, OFF_V, OFF_GA, OFF_Z, OFF_Q, OFF_GC, OFF_GM = 0, 1024, 2048, 3072, 4096, 5120, 6144
OFF_XBC = 9216
OFF_K = OFF_XBC + CONV_DIM
OFF_VV = OFF_K + KV_WIDTH
OFF_DT = OFF_VV + KV_WIDTH
DT_PAD = 256
N_PACK = OFF_DT + DT_PAD
PROJ_TN = 1280

VMEM_LIMIT = 56 * 1024 * 1024
F32 = jnp.float32
BF16 = jnp.bfloat16
NEG_BIG = -1e30


def _cparams(sem):
    return pltpu.CompilerParams(dimension_semantics=sem, vmem_limit_bytes=VMEM_LIMIT)


def _silu(t):
    return t * jax.nn.sigmoid(t)


def _split_dot(x, e, terms=3):
    acc = None
    r = x
    for t in range(terms):
        xt = r.astype(BF16)
        if t + 1 < terms:
            r = r - xt.astype(F32)
        part = jnp.dot(xt, e, preferred_element_type=F32)
        acc = part if acc is None else acc + part
    return acc


def _split_dot_rhs(e, x):
    x1 = x.astype(BF16)
    r1 = x - x1.astype(F32)
    x2 = r1.astype(BF16)
    r2 = r1 - x2.astype(F32)
    x3 = r2.astype(BF16)
    acc = jnp.dot(e, x1, preferred_element_type=F32)
    acc = acc + jnp.dot(e, x2, preferred_element_type=F32)
    return acc + jnp.dot(e, x3, preferred_element_type=F32)


def _mod_kernel(c_ref, w_ref, b_ref, o_ref):
    c = c_ref[...]
    act = _silu(c)
    o_ref[...] = jnp.dot(act, w_ref[...], preferred_element_type=F32,
                         precision=lax.Precision.HIGHEST) + b_ref[...]


def _modulation(c_rows, w_mod, b_mod):
    L = w_mod.shape[0]
    tn = 512
    return pl.pallas_call(
        _mod_kernel,
        out_shape=jax.ShapeDtypeStruct((L, 8, 3 * D_MODEL), F32),
        grid=(L, 3 * D_MODEL // tn),
        in_specs=[pl.BlockSpec((8, D_MODEL), lambda l, j: (0, 0)),
                  pl.BlockSpec((None, D_MODEL, tn), lambda l, j: (l, 0, j)),
                  pl.BlockSpec((None, 1, tn), lambda l, j: (l, 0, j))],
        out_specs=pl.BlockSpec((None, 8, tn), lambda l, j: (l, 0, j)),
        compiler_params=_cparams(("arbitrary", "arbitrary")),
        name="adaln_modulation",
    )(c_rows, w_mod, b_mod)


def _inproj_kernel(x_ref, mod_ref, w_ref, o_ref, dt_ref, h_ref):
    j = pl.program_id(1)

    @pl.when(j == 0)
    def _():
        x = x_ref[...]
        mu = jnp.mean(x, axis=-1, keepdims=True)
        xc = x - mu
        var = jnp.mean(xc * xc, axis=-1, keepdims=True)
        shift = mod_ref[:, 0:D_MODEL]
        scale = mod_ref[:, D_MODEL:2 * D_MODEL]
        h = xc * lax.rsqrt(var + LN_EPS) * (1.0 + scale) + shift
        h_ref[...] = h.astype(BF16)

    acc = jnp.dot(h_ref[...], w_ref[...], preferred_element_type=F32)
    o_ref[...] = acc.astype(o_ref.dtype)

    @pl.when(j == pl.num_programs(1) - 1)
    def _():
        dt_ref[...] = acc[:, PROJ_TN - DT_PAD:]


def _inproj(x2d, mod, w_pack, tm, mod_row):
    T = x2d.shape[0]
    assert OFF_DT + DT_PAD == N_PACK and N_PACK % PROJ_TN == 0
    return pl.pallas_call(
        _inproj_kernel,
        out_shape=(jax.ShapeDtypeStruct((T, N_PACK), BF16),
                   jax.ShapeDtypeStruct((T, DT_PAD), F32)),
        grid=(T // tm, N_PACK // PROJ_TN),
        in_specs=[pl.BlockSpec((tm, D_MODEL), lambda i, j: (i, 0)),
                  pl.BlockSpec((None, 1, 3 * D_MODEL), lambda i, j: (mod_row(i), 0, 0)),
                  pl.BlockSpec((D_MODEL, PROJ_TN), lambda i, j: (0, j))],
        out_specs=(pl.BlockSpec((tm, PROJ_TN), lambda i, j: (i, j)),
                   pl.BlockSpec((tm, DT_PAD), lambda i, j: (i, 0))),
        scratch_shapes=[pltpu.VMEM((tm, D_MODEL), BF16)],
        compiler_params=_cparams(("arbitrary", "arbitrary")),
        name="ln_mod_inproj",
    )(x2d, mod, w_pack)


def _sgu_kernel(u_ref, v_ref, ga_ref, w_ref, b_ref, g_ref, o_ref, *, n_chunks):
    for c in range(n_chunks):
        rows = slice(c * CHUNK, (c + 1) * CHUNK)
        for g in range(A_GROUPS):
            cols = slice(g * A_GROUP_DIM, (g + 1) * A_GROUP_DIM)
            v = v_ref[rows, cols].astype(F32)
            mu = jnp.mean(v, axis=-1, keepdims=True)
            vc = v - mu
            var = jnp.mean(vc * vc, axis=-1, keepdims=True)
            vn = vc * lax.rsqrt(var + LN_EPS) * g_ref[:, cols]
            mixed = jnp.dot(w_ref[g], vn.astype(BF16), preferred_element_type=F32)
            mixed = mixed + b_ref[:, cols]
            y = u_ref[rows, cols].astype(F32) * mixed * _silu(ga_ref[rows, cols].astype(F32))
            o_ref[rows, cols] = y.astype(o_ref.dtype)


def _sgu(p, sgu_w, sgu_bfull, sgu_g, tm):
    T = p.shape[0]
    blk = lambda j: pl.BlockSpec((tm, D_MODEL), lambda i, j=j: (i, j))
    return pl.pallas_call(
        functools.partial(_sgu_kernel, n_chunks=tm // CHUNK),
        out_shape=jax.ShapeDtypeStruct((T, D_MODEL), BF16),
        grid=(T // tm,),
        in_specs=[blk(OFF_U // D_MODEL), blk(OFF_V // D_MODEL), blk(OFF_GA // D_MODEL),
                  pl.BlockSpec((A_GROUPS, CHUNK, CHUNK), lambda i: (0, 0, 0)),
                  pl.BlockSpec((CHUNK, D_MODEL), lambda i: (0, 0)),
                  pl.BlockSpec((1, D_MODEL), lambda i: (0, 0))],
        out_specs=pl.BlockSpec((tm, D_MODEL), lambda i: (i, 0)),
        compiler_params=_cparams(("arbitrary",)),
        name="chunk_sgu",
    )(p, p, p, sgu_w, sgu_bfull, sgu_g)


CONV_HALO = 16


def _conv_kernel(prev_ref, cur_ref, next_ref, w_ref, b_ref, o_ref, ext_ref, *, tiles_per_seq, tc):
    i = pl.program_id(0)
    pos = i % tiles_per_seq
    zeros = jnp.zeros((CONV_HALO, CONV_DIM), F32)
    ext_ref[0:CONV_HALO, :] = jnp.where(pos == 0, zeros, prev_ref[...].astype(F32))
    ext_ref[CONV_HALO:CONV_HALO + tc, :] = cur_ref[...].astype(F32)
    ext_ref[CONV_HALO + tc:, :] = jnp.where(pos == tiles_per_seq - 1, zeros, next_ref[...].astype(F32))
    pad = (D_CONV - 1) // 2
    acc = jnp.zeros((tc, CONV_DIM), F32) + b_ref[...]
    for j in range(D_CONV):
        start = CONV_HALO - pad + j
        acc = acc + ext_ref[start:start + tc, :] * w_ref[j:j + 1, :]
    o_ref[...] = _silu(acc).astype(o_ref.dtype)


def _conv(p, conv_w, conv_b, seq_len, tc):
    T = p.shape[0]
    n_tiles = T // tc
    hb = tc // CONV_HALO
    n_hblocks = T // CONV_HALO
    colb = OFF_XBC // CONV_DIM
    return pl.pallas_call(
        functools.partial(_conv_kernel, tiles_per_seq=seq_len // tc, tc=tc),
        out_shape=jax.ShapeDtypeStruct((T, CONV_DIM), BF16),
        grid=(n_tiles,),
        in_specs=[pl.BlockSpec((CONV_HALO, CONV_DIM), lambda i: (jnp.maximum(i * hb - 1, 0), colb)),
                  pl.BlockSpec((tc, CONV_DIM), lambda i: (i, colb)),
                  pl.BlockSpec((CONV_HALO, CONV_DIM),
                               lambda i: (jnp.minimum((i + 1) * hb, n_hblocks - 1), colb)),
                  pl.BlockSpec((D_CONV, CONV_DIM), lambda i: (0, 0)),
                  pl.BlockSpec((1, CONV_DIM), lambda i: (0, 0))],
        out_specs=pl.BlockSpec((tc, CONV_DIM), lambda i: (i, 0)),
        scratch_shapes=[pltpu.VMEM((tc + 2 * CONV_HALO, CONV_DIM), F32)],
        compiler_params=_cparams(("arbitrary",)),
        name="dwconv_silu",
    )(p, p, p, conv_w, conv_b)


N_CTX_CHUNKS = 2


def _ssd_chunk(x, dt_raw, dtb, alog, dskip, e, ht_ref, d):
    reverse = d == 1
    dt = jax.nn.softplus(dt_raw + dtb)
    a = dt * (-jnp.exp(alog))

    row = lax.broadcasted_iota(jnp.int32, (CHUNK, CHUNK), 0)
    col = lax.broadcasted_iota(jnp.int32, (CHUNK, CHUNK), 1)
    mask = (row <= col) if reverse else (row >= col)
    tri = mask.astype(BF16)
    tri_t = ((row >= col) if reverse else (row <= col)).astype(BF16)

    a_cs = _split_dot_rhs(tri, a)
    a_cs_t = _split_dot(a.T, tri_t)

    a_cs_x = _split_dot(a_cs, e, terms=2)
    dt_x = _split_dot(dt, e, terms=1)
    end = 0 if reverse else CHUNK - 1
    a_tot_x = a_cs_x[end:end + 1, :]

    xs = x[:, 0:D_MODEL].astype(F32)
    xdt = xs * dt_x
    xdt_bf = xdt.astype(BF16)
    xw = (xdt * jnp.exp(a_tot_x - a_cs_x)).astype(BF16)
    exp_acs_x = jnp.exp(a_cs_x)
    chunk_decay_x = jnp.exp(a_tot_x)

    lane = lax.broadcasted_iota(jnp.int32, (CHUNK, 2 * SSM_HEAD_DIM), 1)
    left = lane < SSM_HEAD_DIM
    gw = D_MODEL // SSM_GROUPS
    heads_per_group = SSM_HEADS // SSM_GROUPS
    y_parts = []
    for g in range(SSM_GROUPS):
        b_g = x[:, D_MODEL + g * SSM_STATE:D_MODEL + (g + 1) * SSM_STATE]
        c_g = x[:, D_MODEL + SSM_GROUPS * SSM_STATE + g * SSM_STATE:
                D_MODEL + SSM_GROUPS * SSM_STATE + (g + 1) * SSM_STATE]
        b_bf = b_g
        c_bf = c_g
        cb = lax.dot_general(c_bf, b_bf, (((1,), (1,)), ((), ())), preferred_element_type=F32)
        gcols = slice(g * gw, (g + 1) * gw)
        h_in = ht_ref[g]
        y_inter = jnp.dot(c_bf, h_in.astype(BF16), preferred_element_type=F32) * exp_acs_x[:, gcols]
        b_t = b_g.astype(F32).T.astype(BF16)
        ht_ref[g] = h_in * chunk_decay_x[:, gcols] + jnp.dot(b_t, xw[:, gcols], preferred_element_type=F32)
        for j in range(heads_per_group // 2):
            ms = []
            for k in range(2):
                hh = d * SSM_HEADS + g * heads_per_group + 2 * j + k
                seg = a_cs[:, hh:hh + 1] - a_cs_t[hh:hh + 1, :]
                dec = jnp.exp(jnp.where(mask, seg, NEG_BIG))
                ms.append((cb * dec).astype(BF16))
            lhs = jnp.concatenate(ms, axis=1)
            c0 = (g * heads_per_group + 2 * j) * SSM_HEAD_DIM
            xp = xdt_bf[:, c0:c0 + 2 * SSM_HEAD_DIM]
            zero = jnp.zeros_like(xp)
            rhs = jnp.concatenate([jnp.where(left, xp, zero), jnp.where(left, zero, xp)], axis=0)
            y_pair = jnp.dot(lhs, rhs, preferred_element_type=F32)
            y_parts.append(y_pair + y_inter[:, 2 * j * SSM_HEAD_DIM:(2 * j + 2) * SSM_HEAD_DIM])
    return jnp.concatenate(y_parts, axis=1) + dskip * xs


def _ssd_kernel(xcf_ref, xlf_ref, dtcf_ref, dtlf_ref, xcr_ref, xlr_ref, dtcr_ref, dtlr_ref,
                dtb_ref, alog_ref, dskip_ref, e_ref,
                ycf_ref, ylf_ref, ycr_ref, ylr_ref, ht_ref):
    s = pl.program_id(1)
    is_ctx = s < N_CTX_CHUNKS

    @pl.when(s == 0)
    def _():
        ht_ref[...] = jnp.zeros_like(ht_ref)

    sides = ((xcf_ref, xlf_ref, dtcf_ref, dtlf_ref, ycf_ref, ylf_ref),
             (xcr_ref, xlr_ref, dtcr_ref, dtlr_ref, ycr_ref, ylr_ref))
    ys = []
    for d, (xc_ref, xl_ref, dtc_ref, dtl_ref, _, _) in enumerate(sides):
        x = jnp.where(is_ctx, xc_ref[...], xl_ref[...])
        dt_raw = jnp.where(is_ctx, dtc_ref[:, 0:CHUNK], dtl_ref[:, 0:CHUNK])
        ys.append(_ssd_chunk(x, dt_raw, dtb_ref[...], alog_ref[...], dskip_ref[d], e_ref[d],
                             ht_ref.at[d], d).astype(BF16))

    @pl.when(is_ctx)
    def _():
        for d in range(2):
            sides[d][4][...] = ys[d]

    @pl.when(jnp.logical_not(is_ctx))
    def _():
        for d in range(2):
            sides[d][5][...] = ys[d]


def _ssd(xconv_ctx, xconv_lat, dt_ctx, dt_lat, dtb, alog, dskip_x, e_mats, batch):
    n_lat = xconv_lat.shape[0] // batch // CHUNK
    n_steps = N_CTX_CHUNKS + n_lat

    cidx = (lambda s: jnp.minimum(s, N_CTX_CHUNKS - 1),
            lambda s: jnp.maximum(N_CTX_CHUNKS - 1 - s, 0))
    lidx = (lambda s: jnp.maximum(s - N_CTX_CHUNKS, 0),
            lambda s: jnp.minimum(n_steps - 1 - s, n_lat - 1))

    def specs(width, d):
        return [pl.BlockSpec((CHUNK, width), lambda b, s, d=d: (b * N_CTX_CHUNKS + cidx[d](s), 0)),
                pl.BlockSpec((CHUNK, width), lambda b, s, d=d: (b * n_lat + lidx[d](s), 0))]

    const = lambda shape: pl.BlockSpec(shape, lambda b, s: (0,) * len(shape))
    in_specs, out_specs, args = [], [], []
    for d in range(2):
        in_specs += specs(CONV_DIM, d) + specs(DT_PAD, d)
        args += [xconv_ctx, xconv_lat, dt_ctx, dt_lat]
        out_specs += specs(D_MODEL, d)
    in_specs += [const((1, CHUNK)), const((1, CHUNK)), const((2, 1, D_MODEL)), const((2, CHUNK, D_MODEL))]
    y_shapes = (jax.ShapeDtypeStruct((xconv_ctx.shape[0], D_MODEL), BF16),
                jax.ShapeDtypeStruct((xconv_lat.shape[0], D_MODEL), BF16))
    ycf, ylf, ycr, ylr = pl.pallas_call(
        _ssd_kernel,
        out_shape=y_shapes + y_shapes,
        grid=(batch, n_steps),
        in_specs=in_specs,
        out_specs=tuple(out_specs),
        scratch_shapes=[pltpu.VMEM((2, SSM_GROUPS, SSM_STATE, D_MODEL // SSM_GROUPS), F32)],
        compiler_params=_cparams(("arbitrary", "arbitrary")),
        name="ssd_scan_bidir",
    )(*args, dtb, alog, dskip_x, e_mats)
    return (ycf, ylf), (ycr, ylr)


def _norm_rope(t, g, cos, sin):
    tn = t * lax.rsqrt(jnp.mean(t * t, axis=-1, keepdims=True) + RMS_EPS) * g
    return tn * cos + pltpu.roll(tn, HEAD_DIM // 2, axis=1) * sin


def _kvprep_kernel(k_ref, v_ref, cos_ref, sin_ref, g_ref, ko_ref, vto_ref):
    for h in range(N_KV_HEADS):
        cols = slice(h * HEAD_DIM, (h + 1) * HEAD_DIM)
        kh = _norm_rope(k_ref[:, cols].astype(F32), g_ref[...], cos_ref[...], sin_ref[...])
        ko_ref[h] = kh.astype(BF16)
        vto_ref[h] = v_ref[:, cols].astype(F32).T.astype(BF16)


def _kvprep(p, cos, sin, k_g, batch, tm):
    T = p.shape[0]
    S = T // batch
    nt = S // tm
    return pl.pallas_call(
        _kvprep_kernel,
        out_shape=(jax.ShapeDtypeStruct((batch, N_KV_HEADS, S, HEAD_DIM), BF16),
                   jax.ShapeDtypeStruct((batch, N_KV_HEADS, HEAD_DIM, S), BF16)),
        grid=(batch, nt),
        in_specs=[pl.BlockSpec((tm, KV_WIDTH), lambda b, i: (b * nt + i, OFF_K // KV_WIDTH)),
                  pl.BlockSpec((tm, KV_WIDTH), lambda b, i: (b * nt + i, OFF_VV // KV_WIDTH)),
                  pl.BlockSpec((tm, HEAD_DIM), lambda b, i: (i, 0)),
                  pl.BlockSpec((tm, HEAD_DIM), lambda b, i: (i, 0)),
                  pl.BlockSpec((1, HEAD_DIM), lambda b, i: (0, 0))],
        out_specs=(pl.BlockSpec((None, N_KV_HEADS, tm, HEAD_DIM), lambda b, i: (b, 0, i, 0)),
                   pl.BlockSpec((None, N_KV_HEADS, HEAD_DIM, tm), lambda b, i: (b, 0, 0, i))),
        compiler_params=_cparams(("arbitrary", "arbitrary")),
        name="kv_prep",
    )(p, p, cos, sin, k_g)


LOG2E = 1.4426950408889634


def _attn_kernel(*refs, tq, tk, n_lat_chunks):
    if n_lat_chunks:
        (q_ref, gc_ref, cos_ref, sin_ref, g_ref, kc_ref, vtc_ref, kl_ref, vtl_ref, o_ref,
         qt_ref, acc_ref, s0_ref, s1_ref) = refs
    else:
        q_ref, gc_ref, cos_ref, sin_ref, g_ref, kc_ref, vtc_ref, o_ref, qt_ref, acc_ref = refs
    scale = HEAD_DIM ** -0.5 * LOG2E
    for h in range(GQA_REP):
        cols = slice(h * HEAD_DIM, (h + 1) * HEAD_DIM)
        qh = _norm_rope(q_ref[:, cols].astype(F32), g_ref[...], cos_ref[...], sin_ref[...]) * scale
        qt_ref[:, h * tq:(h + 1) * tq] = qh.T.astype(BF16)

    def scores(k):
        return jnp.dot(k, qt_ref[...], preferred_element_type=F32)

    st = scores(kc_ref[...])
    m = jnp.max(st, axis=0, keepdims=True)
    p = jnp.exp2(st - m)
    l = jnp.sum(p, axis=0, keepdims=True)
    acc_ref[...] = jnp.dot(vtc_ref[...], p.astype(BF16), preferred_element_type=F32)

    if n_lat_chunks:
        def offset(j):
            return j * tk if isinstance(j, int) else pl.multiple_of(j * tk, tk)

        def k_chunk(j):
            return kl_ref[pl.ds(offset(j), tk), :]

        def vt_chunk(j):
            return vtl_ref[:, pl.ds(offset(j), tk)]

        def update(s_ref, vt, m, l):
            st = s_ref[...]
            m_new = jnp.maximum(m, jnp.max(st, axis=0, keepdims=True))
            alpha = jnp.exp2(m - m_new)
            p = jnp.exp2(st - m_new)
            l = alpha * l + jnp.sum(p, axis=0, keepdims=True)
            acc_ref[...] = alpha * acc_ref[...] + jnp.dot(vt, p.astype(BF16), preferred_element_type=F32)
            return m_new, l

        s0_ref[...] = scores(k_chunk(0))

        def pair(jj, carry):
            m, l = carry
            j = 2 * jj
            s1_ref[...] = scores(k_chunk(j + 1))
            m, l = update(s0_ref, vt_chunk(j), m, l)
            s0_ref[...] = scores(k_chunk(j + 2))
            m, l = update(s1_ref, vt_chunk(j + 1), m, l)
            return m, l

        m, l = lax.fori_loop(0, n_lat_chunks // 2 - 1, pair, (m, l))
        j = n_lat_chunks - 2
        s1_ref[...] = scores(k_chunk(j + 1))
        m, l = update(s0_ref, vt_chunk(j), m, l)
        m, l = update(s1_ref, vt_chunk(j + 1), m, l)

    inv = 1.0 / l
    for h in range(GQA_REP):
        cols = slice(h * HEAD_DIM, (h + 1) * HEAD_DIM)
        lanes = slice(h * tq, (h + 1) * tq)
        oh = (acc_ref[:, lanes] * inv[:, lanes]).T
        o_ref[:, cols] = (oh * _silu(gc_ref[:, cols].astype(F32))).astype(o_ref.dtype)


def _attention(p_q, cos, sin, q_g, k_ctx, vt_ctx, k_lat, vt_lat, batch, tq, tk):
    T = p_q.shape[0]
    S = T // batch
    nq = S // tq
    n_ctx = k_ctx.shape[2]
    n_lat_chunks = 0 if k_lat is None else k_lat.shape[2] // tk
    assert n_lat_chunks % 2 == 0
    gw = GQA_REP * HEAD_DIM
    in_specs = [pl.BlockSpec((tq, gw), lambda b, i, g: (b * nq + i, OFF_Q // gw + g)),
                pl.BlockSpec((tq, gw), lambda b, i, g: (b * nq + i, OFF_GC // gw + g)),
                pl.BlockSpec((tq, HEAD_DIM), lambda b, i, g: (i, 0)),
                pl.BlockSpec((tq, HEAD_DIM), lambda b, i, g: (i, 0)),
                pl.BlockSpec((1, HEAD_DIM), lambda b, i, g: (0, 0)),
                pl.BlockSpec((None, None, n_ctx, HEAD_DIM), lambda b, i, g: (b, g, 0, 0)),
                pl.BlockSpec((None, None, HEAD_DIM, n_ctx), lambda b, i, g: (b, g, 0, 0))]
    args = [p_q, p_q, cos, sin, q_g, k_ctx, vt_ctx]
    scratch = [pltpu.VMEM((HEAD_DIM, GQA_REP * tq), BF16), pltpu.VMEM((HEAD_DIM, GQA_REP * tq), F32)]
    if n_lat_chunks:
        n_lat = k_lat.shape[2]
        in_specs += [pl.BlockSpec((None, None, n_lat, HEAD_DIM), lambda b, i, g: (b, g, 0, 0)),
                     pl.BlockSpec((None, None, HEAD_DIM, n_lat), lambda b, i, g: (b, g, 0, 0))]
        args += [k_lat, vt_lat]
        scratch += [pltpu.VMEM((tk, GQA_REP * tq), F32), pltpu.VMEM((tk, GQA_REP * tq), F32)]
    return pl.pallas_call(
        functools.partial(_attn_kernel, tq=tq, tk=tk, n_lat_chunks=n_lat_chunks),
        out_shape=jax.ShapeDtypeStruct((T, D_MODEL), BF16),
        grid=(batch, nq, N_KV_HEADS),
        in_specs=in_specs,
        out_specs=pl.BlockSpec((tq, gw), lambda b, i, g: (b * nq + i, g)),
        scratch_shapes=scratch,
        compiler_params=_cparams(("arbitrary", "arbitrary", "arbitrary")),
        name="gqa_flash_lat" if n_lat_chunks else "gqa_flash_ctx",
    )(*args)


def _merge_kernel(ya_ref, sf_ref, sb_ref, z_ref, yc_ref, gm_ref, x_ref, mod_ref,
                  wb_ref, wo_ref, bo_ref, sg_ref, lg_ref, lb_ref, out_ref):
    s = (sf_ref[...].astype(F32) + sb_ref[...].astype(F32)) * _silu(z_ref[...].astype(F32))
    yb = s * lax.rsqrt(jnp.mean(s * s, axis=-1, keepdims=True) + RMS_EPS) * sg_ref[...]
    gate_of = lambda k: jax.nn.sigmoid(gm_ref[:, k * D_MODEL:(k + 1) * D_MODEL].astype(F32))
    merged = gate_of(0) * jnp.dot(ya_ref[...], wb_ref[0], preferred_element_type=F32)
    merged = merged + gate_of(1) * jnp.dot(yb.astype(BF16), wb_ref[1], preferred_element_type=F32)
    merged = merged + gate_of(2) * jnp.dot(yc_ref[...], wb_ref[2], preferred_element_type=F32)
    out = jnp.dot(merged.astype(BF16), wo_ref[...], preferred_element_type=F32) + bo_ref[...]
    gate = mod_ref[:, 2 * D_MODEL:3 * D_MODEL]
    r = DEEPNORM_ALPHA * x_ref[...] + gate * out
    mu = jnp.mean(r, axis=-1, keepdims=True)
    rc = r - mu
    var = jnp.mean(rc * rc, axis=-1, keepdims=True)
    out_ref[...] = rc * lax.rsqrt(var + LN_EPS) * lg_ref[...] + lb_ref[...]


def _merge(y_a, s_f, s_b, p, y_c, x2d, mod, w_branch, w_out, b_out, ssm_g, ln_g, ln_b, tm, mod_row):
    T = x2d.shape[0]
    row = lambda: pl.BlockSpec((tm, D_MODEL), lambda i: (i, 0))
    pcol = lambda off: pl.BlockSpec((tm, D_MODEL), lambda i: (i, off // D_MODEL))
    vec = lambda: pl.BlockSpec((1, D_MODEL), lambda i: (0, 0))
    once = pl.Buffered(1)
    return pl.pallas_call(
        _merge_kernel,
        out_shape=jax.ShapeDtypeStruct((T, D_MODEL), F32),
        grid=(T // tm,),
        in_specs=[row(), row(), row(), pcol(OFF_Z), row(),
                  pl.BlockSpec((tm, 3 * D_MODEL), lambda i: (i, OFF_GM // (3 * D_MODEL))),
                  row(),
                  pl.BlockSpec((None, 1, 3 * D_MODEL), lambda i: (mod_row(i), 0, 0)),
                  pl.BlockSpec((3, D_MODEL, D_MODEL), lambda i: (0, 0, 0), pipeline_mode=once),
                  pl.BlockSpec((D_MODEL, D_MODEL), lambda i: (0, 0), pipeline_mode=once),
                  vec(), vec(), vec(), vec()],
        out_specs=row(),
        compiler_params=_cparams(("arbitrary",)),
        name="merge_out_deepnorm",
    )(y_a, s_f, s_b, p, y_c, p, x2d, mod, w_branch, w_out, b_out, ssm_g, ln_g, ln_b)


def _deinterleave_cols(w, n_heads):
    r = w.shape[0]
    return w.reshape(r, n_heads, HEAD_DIM // 2, 2).transpose(0, 1, 3, 2).reshape(r, n_heads * HEAD_DIM)


def _pack_w_in(w):
    o = 0
    seg = {}
    for name, n in (("u", 1024), ("v", 1024), ("ga", 1024), ("z", 1024), ("xbc", CONV_DIM),
                    ("dt", 2 * SSM_HEADS), ("q", 1024), ("k", KV_WIDTH), ("vv", KV_WIDTH),
                    ("gc", 1024), ("gm", 3 * D_MODEL)):
        seg[name] = w[:, o:o + n]
        o += n
    pad = jnp.zeros((w.shape[0], DT_PAD - 2 * SSM_HEADS), w.dtype)
    packed = jnp.concatenate(
        [seg["u"], seg["v"], seg["ga"], seg["z"], _deinterleave_cols(seg["q"], N_Q_HEADS),
         seg["gc"], seg["gm"], seg["xbc"], _deinterleave_cols(seg["k"], N_KV_HEADS), seg["vv"],
         seg["dt"], pad], axis=1)
    return packed.astype(BF16)


def _rope_tables(n_tokens):
    rows = n_tokens // GRID_W
    row = jnp.broadcast_to(jnp.arange(rows)[:, None], (rows, GRID_W)).reshape(-1)
    col = jnp.broadcast_to(jnp.arange(GRID_W)[None, :], (rows, GRID_W)).reshape(-1)
    inv = ROPE_THETA ** (-jnp.arange(ROPE_PAIRS_PER_AXIS, dtype=F32) / ROPE_PAIRS_PER_AXIS)
    ang = jnp.concatenate([row[:, None] * inv, col[:, None] * inv], axis=-1)
    cos, sin = jnp.cos(ang), jnp.sin(ang)
    return jnp.concatenate([cos, cos], axis=-1), jnp.concatenate([-sin, sin], axis=-1)


def _pad_lanes(v, n):
    return jnp.concatenate([v, jnp.zeros((n - v.shape[0],), v.dtype)])[None, :]


def kernel(x, c, ctx, c_ctx, w_mod, b_mod, w_in, sgu_w, sgu_b, sgu_norm_g, conv_w, conv_b, dt_bias,
           a_log, d_skip, ssm_norm_g, q_norm_g, k_norm_g, w_branch, w_out, b_out, ln_g, ln_b):
    batch, seq, _ = x.shape
    n_ctx = ctx.shape[1]
    assert n_ctx == N_CTX_CHUNKS * CHUNK and seq % 1024 == 0

    cos_l, sin_l = _rope_tables(seq)
    cos_c = jnp.ones((n_ctx, HEAD_DIM), F32)
    sin_c = jnp.zeros((n_ctx, HEAD_DIM), F32)

    c_rows = jnp.concatenate([c, c_ctx[None, :], jnp.zeros((8 - batch - 1, D_MODEL), F32)], axis=0)
    mod_all = _modulation(c_rows, w_mod, b_mod[:, None, :])
    ctx_row = batch

    hid = jnp.arange(CHUNK)[:, None]
    ch = jnp.arange(D_MODEL)[None, :] // SSM_HEAD_DIM
    e_mats = jnp.stack([(hid == ch + d * SSM_HEADS).astype(BF16) for d in range(2)])

    lat_per_tile = 1024
    tiles_per_batch = seq // lat_per_tile
    xl = x.reshape(batch * seq, D_MODEL)
    xc = ctx.reshape(batch * n_ctx, D_MODEL)

    for l in range(DEPTH):
        ctx_out = l < DEPTH - 1
        mod = mod_all[l][:, None, :]
        w_pack = _pack_w_in(w_in[l])
        wb = w_branch[l].astype(BF16)
        wo = w_out[l].astype(BF16)
        perm_g = lambda g: g.reshape(HEAD_DIM // 2, 2).T.reshape(1, HEAD_DIM)
        q_g, k_g = perm_g(q_norm_g[l]), perm_g(k_norm_g[l])
        sgu_wb = sgu_w[l].astype(BF16)
        sgu_bfull = jnp.repeat(sgu_b[l].T, A_GROUP_DIM, axis=1)
        sgu_g = sgu_norm_g[l][None, :]
        dtb = _pad_lanes(dt_bias[l].reshape(-1), CHUNK)
        alog = _pad_lanes(a_log[l].reshape(-1), CHUNK)

        lat_row = lambda i: i // tiles_per_batch
        p_lat, dt_lat = _inproj(xl, mod, w_pack, lat_per_tile, lat_row)
        p_ctx, dt_ctx = _inproj(xc, mod, w_pack, n_ctx, lambda i: ctx_row)

        xconv_lat = _conv(p_lat, conv_w[l], conv_b[l][None, :], seq, 512)
        xconv_ctx = _conv(p_ctx, conv_w[l], conv_b[l][None, :], n_ctx, n_ctx)
        dskip_x = jnp.repeat(d_skip[l], SSM_HEAD_DIM, axis=1)[:, None, :]
        ssd = _ssd(xconv_ctx, xconv_lat, dt_ctx, dt_lat, dtb, alog, dskip_x, e_mats, batch)

        k_ctx, vt_ctx = _kvprep(p_ctx, cos_c, sin_c, k_g, batch, n_ctx)
        k_lat, vt_lat = _kvprep(p_lat, cos_l, sin_l, k_g, batch, 512)
        o_lat = _attention(p_lat, cos_l, sin_l, q_g, k_ctx, vt_ctx, k_lat, vt_lat, batch, 256, 512)

        ya_lat = _sgu(p_lat, sgu_wb, sgu_bfull, sgu_g, 512)

        vecs = (b_out[l][None, :], ssm_norm_g[l][None, :], ln_g[l][None, :], ln_b[l][None, :])
        merge_tm = 512
        merge_tiles_per_batch = seq // merge_tm
        xl_new = _merge(ya_lat, ssd[0][1], ssd[1][1], p_lat, o_lat, xl, mod, wb, wo, *vecs,
                        merge_tm, lambda i: i // merge_tiles_per_batch)
        if ctx_out:
            o_ctx = _attention(p_ctx, cos_c, sin_c, q_g, k_ctx, vt_ctx, None, None, batch, n_ctx, 0)
            ya_ctx = _sgu(p_ctx, sgu_wb, sgu_bfull, sgu_g, n_ctx)
            xc = _merge(ya_ctx, ssd[0][0], ssd[1][0], p_ctx, o_ctx, xc, mod, wb, wo, *vecs,
                        n_ctx, lambda i: ctx_row)
        xl = xl_new

    return xl.reshape(batch, seq, D_MODEL)
```

```python
import functools
import math

import jax
import jax.numpy as jnp
from jax import lax
from jax.experimental import pallas as pl
from jax.experimental.pallas import tpu as pltpu

D_MODEL = 1024
DEPTH = 2
GRID_W = 64
CHUNK = 128
A_GROUPS = 4
A_GROUP_DIM = D_MODEL // A_GROUPS
SSM_HEAD_DIM = 64
SSM_HEADS = 16
SSM_GROUPS = 2
SSM_STATE = 128
D_CONV = 5
CONV_DIM = D_MODEL + 2 * SSM_GROUPS * SSM_STATE
HEAD_DIM = 128
N_Q_HEADS = 8
N_KV_HEADS = 2
GQA_REP = N_Q_HEADS // N_KV_HEADS
KV_WIDTH = N_KV_HEADS * HEAD_DIM
ROPE_THETA = 10000.0
ROPE_PAIRS_PER_AXIS = HEAD_DIM // 4
DEEPNORM_ALPHA = (2 * DEPTH) ** 0.25
LN_EPS = 1e-6
RMS_EPS = 1e-6

OFF_U, OFF_V, OFF_GA, OFF_Z, OFF_Q, OFF_GC, OFF_GM = 0, 1024, 2048, 3072, 4096, 5120, 6144
OFF_XBC = 9216
OFF_K = OFF_XBC + CONV_DIM
OFF_VV = OFF_K + KV_WIDTH
OFF_DT = OFF_VV + KV_WIDTH
DT_PAD = 256
N_PACK = OFF_DT + DT_PAD
PROJ_TN = 1280

BF16_SUBLANES = 16
VT_ROWS = HEAD_DIM + BF16_SUBLANES

VMEM_LIMIT = 56 * 1024 * 1024
F32 = jnp.float32
BF16 = jnp.bfloat16
NEG_BIG = -1e30


def _cparams(sem):
    return pltpu.CompilerParams(dimension_semantics=sem, vmem_limit_bytes=VMEM_LIMIT)


def _silu(t):
    return t * jax.nn.sigmoid(t)


def _split_dot(x, e, terms=3):
    acc = None
    r = x
    for t in range(terms):
        xt = r.astype(BF16)
        if t + 1 < terms:
            r = r - xt.astype(F32)
        part = jnp.dot(xt, e, preferred_element_type=F32)
        acc = part if acc is None else acc + part
    return acc


def _split_dot_rhs(e, x):
    x1 = x.astype(BF16)
    r1 = x - x1.astype(F32)
    x2 = r1.astype(BF16)
    r2 = r1 - x2.astype(F32)
    x3 = r2.astype(BF16)
    acc = jnp.dot(e, x1, preferred_element_type=F32)
    acc = acc + jnp.dot(e, x2, preferred_element_type=F32)
    return acc + jnp.dot(e, x3, preferred_element_type=F32)


def _mod_kernel(c_ref, w_ref, b_ref, o_ref):
    c = c_ref[...]
    act = _silu(c)
    o_ref[...] = jnp.dot(act, w_ref[...], preferred_element_type=F32,
                         precision=lax.Precision.HIGHEST) + b_ref[...]


def _modulation(c_rows, w_mod, b_mod):
    L = w_mod.shape[0]
    tn = 512
    return pl.pallas_call(
        _mod_kernel,
        out_shape=jax.ShapeDtypeStruct((L, 8, 3 * D_MODEL), F32),
        grid=(L, 3 * D_MODEL // tn),
        in_specs=[pl.BlockSpec((8, D_MODEL), lambda l, j: (0, 0)),
                  pl.BlockSpec((None, D_MODEL, tn), lambda l, j: (l, 0, j)),
                  pl.BlockSpec((None, 1, tn), lambda l, j: (l, 0, j))],
        out_specs=pl.BlockSpec((None, 8, tn), lambda l, j: (l, 0, j)),
        compiler_params=_cparams(("arbitrary", "arbitrary")),
        name="adaln_modulation",
    )(c_rows, w_mod, b_mod)


def _inproj_kernel(x_ref, mod_ref, w_ref, o_ref, dt_ref, h_ref):
    j = pl.program_id(1)

    @pl.when(j == 0)
    def _():
        x = x_ref[...]
        mu = jnp.mean(x, axis=-1, keepdims=True)
        xc = x - mu
        var = jnp.mean(xc * xc, axis=-1, keepdims=True)
        shift = mod_ref[:, 0:D_MODEL]
        scale = mod_ref[:, D_MODEL:2 * D_MODEL]
        h = xc * lax.rsqrt(var + LN_EPS) * (1.0 + scale) + shift
        h_ref[...] = h.astype(BF16)

    acc = jnp.dot(h_ref[...], w_ref[...], preferred_element_type=F32)
    o_ref[...] = acc.astype(o_ref.dtype)

    @pl.when(j == pl.num_programs(1) - 1)
    def _():
        dt_ref[...] = acc[:, PROJ_TN - DT_PAD:]


def _inproj(x2d, mod, w_pack, tm, mod_row):
    T = x2d.shape[0]
    assert OFF_DT + DT_PAD == N_PACK and N_PACK % PROJ_TN == 0
    return pl.pallas_call(
        _inproj_kernel,
        out_shape=(jax.ShapeDtypeStruct((T, N_PACK), BF16),
                   jax.ShapeDtypeStruct((T, DT_PAD), F32)),
        grid=(T // tm, N_PACK // PROJ_TN),
        in_specs=[pl.BlockSpec((tm, D_MODEL), lambda i, j: (i, 0)),
                  pl.BlockSpec((None, 1, 3 * D_MODEL), lambda i, j: (mod_row(i), 0, 0)),
                  pl.BlockSpec((D_MODEL, PROJ_TN), lambda i, j: (0, j))],
        out_specs=(pl.BlockSpec((tm, PROJ_TN), lambda i, j: (i, j)),
                   pl.BlockSpec((tm, DT_PAD), lambda i, j: (i, 0))),
        scratch_shapes=[pltpu.VMEM((tm, D_MODEL), BF16)],
        compiler_params=_cparams(("arbitrary", "arbitrary")),
        name="ln_mod_inproj",
    )(x2d, mod, w_pack)


def _sgu_kernel(u_ref, v_ref, ga_ref, w_ref, b_ref, g_ref, o_ref, *, n_chunks):
    for c in range(n_chunks):
        rows = slice(c * CHUNK, (c + 1) * CHUNK)
        for g in range(A_GROUPS):
            cols = slice(g * A_GROUP_DIM, (g + 1) * A_GROUP_DIM)
            v = v_ref[rows, cols].astype(F32)
            mu = jnp.mean(v, axis=-1, keepdims=True)
            vc = v - mu
            var = jnp.mean(vc * vc, axis=-1, keepdims=True)
            vn = vc * lax.rsqrt(var + LN_EPS) * g_ref[:, cols]
            mixed = jnp.dot(w_ref[g], vn.astype(BF16), preferred_element_type=F32)
            mixed = mixed + b_ref[:, cols]
            y = u_ref[rows, cols].astype(F32) * mixed * _silu(ga_ref[rows, cols].astype(F32))
            o_ref[rows, cols] = y.astype(o_ref.dtype)


def _sgu(p, sgu_w, sgu_bfull, sgu_g, tm):
    T = p.shape[0]
    blk = lambda j: pl.BlockSpec((tm, D_MODEL), lambda i, j=j: (i, j))
    return pl.pallas_call(
        functools.partial(_sgu_kernel, n_chunks=tm // CHUNK),
        out_shape=jax.ShapeDtypeStruct((T, D_MODEL), BF16),
        grid=(T // tm,),
        in_specs=[blk(OFF_U // D_MODEL), blk(OFF_V // D_MODEL), blk(OFF_GA // D_MODEL),
                  pl.BlockSpec((A_GROUPS, CHUNK, CHUNK), lambda i: (0, 0, 0)),
                  pl.BlockSpec((CHUNK, D_MODEL), lambda i: (0, 0)),
                  pl.BlockSpec((1, D_MODEL), lambda i: (0, 0))],
        out_specs=pl.BlockSpec((tm, D_MODEL), lambda i: (i, 0)),
        compiler_params=_cparams(("arbitrary",)),
        name="chunk_sgu",
    )(p, p, p, sgu_w, sgu_bfull, sgu_g)


CONV_HALO = 16


def _conv_kernel(prev_ref, cur_ref, next_ref, w_ref, b_ref, o_ref, ext_ref, *, tiles_per_seq, tc):
    i = pl.program_id(0)
    pos = i % tiles_per_seq
    zeros = jnp.zeros((CONV_HALO, CONV_DIM), F32)
    ext_ref[0:CONV_HALO, :] = jnp.where(pos == 0, zeros, prev_ref[...].astype(F32))
    ext_ref[CONV_HALO:CONV_HALO + tc, :] = cur_ref[...].astype(F32)
    ext_ref[CONV_HALO + tc:, :] = jnp.where(pos == tiles_per_seq - 1, zeros, next_ref[...].astype(F32))
    pad = (D_CONV - 1) // 2
    acc = jnp.zeros((tc, CONV_DIM), F32) + b_ref[...]
    for j in range(D_CONV):
        start = CONV_HALO - pad + j
        acc = acc + ext_ref[start:start + tc, :] * w_ref[j:j + 1, :]
    o_ref[...] = _silu(acc).astype(o_ref.dtype)


def _conv(p, conv_w, conv_b, seq_len, tc):
    T = p.shape[0]
    n_tiles = T // tc
    hb = tc // CONV_HALO
    n_hblocks = T // CONV_HALO
    colb = OFF_XBC // CONV_DIM
    return pl.pallas_call(
        functools.partial(_conv_kernel, tiles_per_seq=seq_len // tc, tc=tc),
        out_shape=jax.ShapeDtypeStruct((T, CONV_DIM), BF16),
        grid=(n_tiles,),
        in_specs=[pl.BlockSpec((CONV_HALO, CONV_DIM), lambda i: (jnp.maximum(i * hb - 1, 0), colb)),
                  pl.BlockSpec((tc, CONV_DIM), lambda i: (i, colb)),
                  pl.BlockSpec((CONV_HALO, CONV_DIM),
                               lambda i: (jnp.minimum((i + 1) * hb, n_hblocks - 1), colb)),
                  pl.BlockSpec((D_CONV, CONV_DIM), lambda i: (0, 0)),
                  pl.BlockSpec((1, CONV_DIM), lambda i: (0, 0))],
        out_specs=pl.BlockSpec((tc, CONV_DIM), lambda i: (i, 0)),
        scratch_shapes=[pltpu.VMEM((tc + 2 * CONV_HALO, CONV_DIM), F32)],
        compiler_params=_cparams(("arbitrary",)),
        name="dwconv_silu",
    )(p, p, p, conv_w, conv_b)


N_CTX_CHUNKS = 2


SSD_BLOCK = 512


def _ssd_chunk(x_cols, dt_raw, dtb, alog, dskip_ref, e_ref, ht_ref, y_ref, d):
    reverse = d == 1
    dt = jax.nn.softplus(dt_raw + dtb)
    a = dt * (-jnp.exp(alog))

    row = lax.broadcasted_iota(jnp.int32, (CHUNK, CHUNK), 0)
    col = lax.broadcasted_iota(jnp.int32, (CHUNK, CHUNK), 1)
    mask = (row <= col) if reverse else (row >= col)
    tri = mask.astype(BF16)
    tri_t = ((row >= col) if reverse else (row <= col)).astype(BF16)

    a_cs = _split_dot_rhs(tri, a)
    a_cs_t = _split_dot(a.T, tri_t)
    end = 0 if reverse else CHUNK - 1

    lane = lax.broadcasted_iota(jnp.int32, (CHUNK, 2 * SSM_HEAD_DIM), 1)
    left = lane < SSM_HEAD_DIM
    gw = D_MODEL // SSM_GROUPS
    heads_per_group = SSM_HEADS // SSM_GROUPS
    heads_per_block = SSD_BLOCK // SSM_HEAD_DIM
    for g in range(SSM_GROUPS):
        b_g = x_cols(D_MODEL + g * SSM_STATE, D_MODEL + (g + 1) * SSM_STATE)
        c_g = x_cols(D_MODEL + (SSM_GROUPS + g) * SSM_STATE, D_MODEL + (SSM_GROUPS + g + 1) * SSM_STATE)
        cb = lax.dot_general(c_g, b_g, (((1,), (1,)), ((), ())), preferred_element_type=F32)
        b_t = b_g.astype(F32).T.astype(BF16)
        for blk in range(gw // SSD_BLOCK):
            c0 = g * gw + blk * SSD_BLOCK
            cols = slice(c0, c0 + SSD_BLOCK)
            gcols = slice(blk * SSD_BLOCK, (blk + 1) * SSD_BLOCK)
            e_blk = e_ref[:, cols]
            a_cs_x = _split_dot(a_cs, e_blk, terms=2)
            dt_x = _split_dot(dt, e_blk, terms=1)
            a_tot_x = a_cs_x[end:end + 1, :]
            xs = x_cols(c0, c0 + SSD_BLOCK).astype(F32)
            xdt = xs * dt_x
            xdt_bf = xdt.astype(BF16)
            xw = (xdt * jnp.exp(a_tot_x - a_cs_x)).astype(BF16)
            h_in = ht_ref[g, :, gcols]
            y_blk = jnp.dot(c_g, h_in.astype(BF16), preferred_element_type=F32) * jnp.exp(a_cs_x)
            ht_ref[g, :, gcols] = h_in * jnp.exp(a_tot_x) + jnp.dot(b_t, xw, preferred_element_type=F32)
            y_pairs = []
            for j in range(heads_per_block // 2):
                ms = []
                for k in range(2):
                    hh = d * SSM_HEADS + g * heads_per_group + blk * heads_per_block + 2 * j + k
                    seg = a_cs[:, hh:hh + 1] - a_cs_t[hh:hh + 1, :]
                    dec = jnp.exp(jnp.where(mask, seg, NEG_BIG))
                    ms.append((cb * dec).astype(BF16))
                lhs = jnp.concatenate(ms, axis=1)
                xp = xdt_bf[:, 2 * j * SSM_HEAD_DIM:(2 * j + 2) * SSM_HEAD_DIM]
                zero = jnp.zeros_like(xp)
                rhs = jnp.concatenate([jnp.where(left, xp, zero), jnp.where(left, zero, xp)], axis=0)
                y_pairs.append(jnp.dot(lhs, rhs, preferred_element_type=F32))
            y_blk = y_blk + jnp.concatenate(y_pairs, axis=1) + dskip_ref[:, cols] * xs
            y_ref[:, cols] = y_blk.astype(y_ref.dtype)


def _ssd_kernel(xcf_ref, xlf_ref, dtcf_ref, dtlf_ref, xcr_ref, xlr_ref, dtcr_ref, dtlr_ref,
                dtb_ref, alog_ref, dskip_ref, e_ref,
                ycf_ref, ylf_ref, ycr_ref, ylr_ref, ht_ref, y_ref):
    s = pl.program_id(1)
    is_ctx = s < N_CTX_CHUNKS

    @pl.when(s == 0)
    def _():
        ht_ref[...] = jnp.zeros_like(ht_ref)

    sides = ((xcf_ref, xlf_ref, dtcf_ref, dtlf_ref, ycf_ref, ylf_ref),
             (xcr_ref, xlr_ref, dtcr_ref, dtlr_ref, ycr_ref, ylr_ref))
    for d, (xc_ref, xl_ref, dtc_ref, dtl_ref, _, _) in enumerate(sides):
        x_cols = lambda c0, c1, xc_ref=xc_ref, xl_ref=xl_ref: jnp.where(
            is_ctx, xc_ref[:, c0:c1], xl_ref[:, c0:c1])
        dt_raw = jnp.where(is_ctx, dtc_ref[:, 0:CHUNK], dtl_ref[:, 0:CHUNK])
        _ssd_chunk(x_cols, dt_raw, dtb_ref[...], alog_ref[...], dskip_ref.at[d], e_ref.at[d],
                   ht_ref.at[d], y_ref.at[d], d)

    @pl.when(is_ctx)
    def _():
        for d in range(2):
            sides[d][4][...] = y_ref[d]

    @pl.when(jnp.logical_not(is_ctx))
    def _():
        for d in range(2):
            sides[d][5][...] = y_ref[d]


def _ssd(xconv_ctx, xconv_lat, dt_ctx, dt_lat, dtb, alog, dskip_x, e_mats, batch):
    n_lat = xconv_lat.shape[0] // batch // CHUNK
    n_steps = N_CTX_CHUNKS + n_lat

    cidx = (lambda s: jnp.minimum(s, N_CTX_CHUNKS - 1),
            lambda s: jnp.maximum(N_CTX_CHUNKS - 1 - s, 0))
    lidx = (lambda s: jnp.maximum(s - N_CTX_CHUNKS, 0),
            lambda s: jnp.minimum(n_steps - 1 - s, n_lat - 1))

    def specs(width, d):
        return [pl.BlockSpec((CHUNK, width), lambda b, s, d=d: (b * N_CTX_CHUNKS + cidx[d](s), 0)),
                pl.BlockSpec((CHUNK, width), lambda b, s, d=d: (b * n_lat + lidx[d](s), 0))]

    const = lambda shape: pl.BlockSpec(shape, lambda b, s: (0,) * len(shape))
    in_specs, out_specs, args = [], [], []
    for d in range(2):
        in_specs += specs(CONV_DIM, d) + specs(DT_PAD, d)
        args += [xconv_ctx, xconv_lat, dt_ctx, dt_lat]
        out_specs += specs(D_MODEL, d)
    in_specs += [const((1, CHUNK)), const((1, CHUNK)), const((2, 1, D_MODEL)), const((2, CHUNK, D_MODEL))]
    y_shapes = (jax.ShapeDtypeStruct((xconv_ctx.shape[0], D_MODEL), BF16),
                jax.ShapeDtypeStruct((xconv_lat.shape[0], D_MODEL), BF16))
    ycf, ylf, ycr, ylr = pl.pallas_call(
        _ssd_kernel,
        out_shape=y_shapes + y_shapes,
        grid=(batch, n_steps),
        in_specs=in_specs,
        out_specs=tuple(out_specs),
        scratch_shapes=[pltpu.VMEM((2, SSM_GROUPS, SSM_STATE, D_MODEL // SSM_GROUPS), F32),
                        pltpu.VMEM((2, CHUNK, D_MODEL), BF16)],
        compiler_params=_cparams(("arbitrary", "arbitrary")),
        name="ssd_scan_bidir",
    )(*args, dtb, alog, dskip_x, e_mats)
    return (ycf, ylf), (ycr, ylr)


def _norm_rope(t, g, cos, sin):
    tn = t * lax.rsqrt(jnp.mean(t * t, axis=-1, keepdims=True) + RMS_EPS) * g
    lane = lax.broadcasted_iota(jnp.int32, tn.shape, 1)
    partner = jnp.where(lane % 2 == 0, pltpu.roll(tn, HEAD_DIM - 1, axis=1), pltpu.roll(tn, 1, axis=1))
    return tn * cos + partner * sin


def _kvprep_kernel(k_ref, v_ref, cos_ref, sin_ref, g_ref, ko_ref, vto_ref):
    for h in range(N_KV_HEADS):
        cols = slice(h * HEAD_DIM, (h + 1) * HEAD_DIM)
        kh = _norm_rope(k_ref[:, cols].astype(F32), g_ref[...], cos_ref[...], sin_ref[...])
        ko_ref[h] = kh.astype(BF16)
        vto_ref[h, 0:HEAD_DIM, :] = v_ref[:, cols].astype(F32).T.astype(BF16)
        vto_ref[h, HEAD_DIM:, :] = jnp.ones((VT_ROWS - HEAD_DIM, k_ref.shape[0]), BF16)


def _kvprep(p, cos, sin, k_g, batch, tm):
    T = p.shape[0]
    S = T // batch
    nt = S // tm
    return pl.pallas_call(
        _kvprep_kernel,
        out_shape=(jax.ShapeDtypeStruct((batch, N_KV_HEADS, S, HEAD_DIM), BF16),
                   jax.ShapeDtypeStruct((batch, N_KV_HEADS, VT_ROWS, S), BF16)),
        grid=(batch, nt),
        in_specs=[pl.BlockSpec((tm, KV_WIDTH), lambda b, i: (b * nt + i, OFF_K // KV_WIDTH)),
                  pl.BlockSpec((tm, KV_WIDTH), lambda b, i: (b * nt + i, OFF_VV // KV_WIDTH)),
                  pl.BlockSpec((tm, HEAD_DIM), lambda b, i: (i, 0)),
                  pl.BlockSpec((tm, HEAD_DIM), lambda b, i: (i, 0)),
                  pl.BlockSpec((1, HEAD_DIM), lambda b, i: (0, 0))],
        out_specs=(pl.BlockSpec((None, N_KV_HEADS, tm, HEAD_DIM), lambda b, i: (b, 0, i, 0)),
                   pl.BlockSpec((None, N_KV_HEADS, VT_ROWS, tm), lambda b, i: (b, 0, 0, i))),
        compiler_params=_cparams(("arbitrary", "arbitrary")),
        name="kv_prep",
    )(p, p, cos, sin, k_g)


LOG2E = 1.4426950408889634


def _attn_kernel(*refs, tq, tk, n_lat_chunks):
    if n_lat_chunks:
        (q_ref, gc_ref, cos_ref, sin_ref, g_ref, kc_ref, vtc_ref, kl_ref, vtl_ref, o_ref,
         qt_ref, acc_ref, s0_ref, s1_ref) = refs
    else:
        q_ref, gc_ref, cos_ref, sin_ref, g_ref, kc_ref, vtc_ref, o_ref, qt_ref, acc_ref = refs
    scale = HEAD_DIM ** -0.5 * LOG2E
    for h in range(GQA_REP):
        cols = slice(h * HEAD_DIM, (h + 1) * HEAD_DIM)
        qh = _norm_rope(q_ref[:, cols].astype(F32), g_ref[...], cos_ref[...], sin_ref[...]) * scale
        qt_ref[:, h * tq:(h + 1) * tq] = qh.T.astype(BF16)

    def scores(k):
        return jnp.dot(k, qt_ref[...], preferred_element_type=F32)

    st = scores(kc_ref[...])
    m = jnp.max(st, axis=0, keepdims=True)
    p = jnp.exp2(st - m)
    acc_ref[...] = jnp.dot(vtc_ref[...], p.astype(BF16), preferred_element_type=F32)

    if n_lat_chunks:
        def offset(j):
            return j * tk if isinstance(j, int) else pl.multiple_of(j * tk, tk)

        def k_chunk(j):
            return kl_ref[pl.ds(offset(j), tk), :]

        def vt_chunk(j):
            return vtl_ref[:, pl.ds(offset(j), tk)]

        def scores_into(s_ref, k):
            st = scores(k)
            s_ref[...] = st
            return jnp.max(st, axis=0, keepdims=True)

        def update(s_ref, vt, m, cmax):
            m_new = jnp.maximum(m, cmax)
            alpha = jnp.exp2(m - m_new)
            p = jnp.exp2(s_ref[...] - m_new)
            acc_ref[...] = alpha * acc_ref[...] + jnp.dot(vt, p.astype(BF16), preferred_element_type=F32)
            return m_new

        cm0 = scores_into(s0_ref, k_chunk(0))

        def pair(jj, carry):
            m, cm0 = carry
            j = 2 * jj
            cm1 = scores_into(s1_ref, k_chunk(j + 1))
            m = update(s0_ref, vt_chunk(j), m, cm0)
            cm0 = scores_into(s0_ref, k_chunk(j + 2))
            m = update(s1_ref, vt_chunk(j + 1), m, cm1)
            return m, cm0

        m, cm0 = lax.fori_loop(0, n_lat_chunks // 2 - 1, pair, (m, cm0))
        j = n_lat_chunks - 2
        cm1 = scores_into(s1_ref, k_chunk(j + 1))
        m = update(s0_ref, vt_chunk(j), m, cm0)
        m = update(s1_ref, vt_chunk(j + 1), m, cm1)

    inv = 1.0 / acc_ref[HEAD_DIM:HEAD_DIM + 1, :]
    for h in range(GQA_REP):
        cols = slice(h * HEAD_DIM, (h + 1) * HEAD_DIM)
        lanes = slice(h * tq, (h + 1) * tq)
        oh = (acc_ref[0:HEAD_DIM, lanes] * inv[:, lanes]).T
        o_ref[:, cols] = (oh * _silu(gc_ref[:, cols].astype(F32))).astype(o_ref.dtype)


def _attention(p_q, cos, sin, q_g, k_ctx, vt_ctx, k_lat, vt_lat, batch, tq, tk):
    T = p_q.shape[0]
    S = T // batch
    nq = S // tq
    n_ctx = k_ctx.shape[2]
    n_lat_chunks = 0 if k_lat is None else k_lat.shape[2] // tk
    assert n_lat_chunks % 2 == 0
    gw = GQA_REP * HEAD_DIM
    in_specs = [pl.BlockSpec((tq, gw), lambda b, i, g: (b * nq + i, OFF_Q // gw + g)),
                pl.BlockSpec((tq, gw), lambda b, i, g: (b * nq + i, OFF_GC // gw + g)),
                pl.BlockSpec((tq, HEAD_DIM), lambda b, i, g: (i, 0)),
                pl.BlockSpec((tq, HEAD_DIM), lambda b, i, g: (i, 0)),
                pl.BlockSpec((1, HEAD_DIM), lambda b, i, g: (0, 0)),
                pl.BlockSpec((None, None, n_ctx, HEAD_DIM), lambda b, i, g: (b, g, 0, 0)),
                pl.BlockSpec((None, None, VT_ROWS, n_ctx), lambda b, i, g: (b, g, 0, 0))]
    args = [p_q, p_q, cos, sin, q_g, k_ctx, vt_ctx]
    scratch = [pltpu.VMEM((HEAD_DIM, GQA_REP * tq), BF16), pltpu.VMEM((VT_ROWS, GQA_REP * tq), F32)]
    if n_lat_chunks:
        n_lat = k_lat.shape[2]
        in_specs += [pl.BlockSpec((None, None, n_lat, HEAD_DIM), lambda b, i, g: (b, g, 0, 0)),
                     pl.BlockSpec((None, None, VT_ROWS, n_lat), lambda b, i, g: (b, g, 0, 0))]
        args += [k_lat, vt_lat]
        scratch += [pltpu.VMEM((tk, GQA_REP * tq), F32), pltpu.VMEM((tk, GQA_REP * tq), F32)]
    return pl.pallas_call(
        functools.partial(_attn_kernel, tq=tq, tk=tk, n_lat_chunks=n_lat_chunks),
        out_shape=jax.ShapeDtypeStruct((T, D_MODEL), BF16),
        grid=(batch, nq, N_KV_HEADS),
        in_specs=in_specs,
        out_specs=pl.BlockSpec((tq, gw), lambda b, i, g: (b * nq + i, g)),
        scratch_shapes=scratch,
        compiler_params=_cparams(("arbitrary", "arbitrary", "arbitrary")),
        name="gqa_flash_lat" if n_lat_chunks else "gqa_flash_ctx",
    )(*args)


def _merge_kernel(ya_ref, sf_ref, sb_ref, z_ref, yc_ref, gm_ref, x_ref, mod_ref,
                  wb_ref, wo_ref, bo_ref, sg_ref, lg_ref, lb_ref, out_ref):
    s = (sf_ref[...].astype(F32) + sb_ref[...].astype(F32)) * _silu(z_ref[...].astype(F32))
    yb = s * lax.rsqrt(jnp.mean(s * s, axis=-1, keepdims=True) + RMS_EPS) * sg_ref[...]
    gate_of = lambda k: jax.nn.sigmoid(gm_ref[:, k * D_MODEL:(k + 1) * D_MODEL].astype(F32))
    merged = gate_of(0) * jnp.dot(ya_ref[...], wb_ref[0], preferred_element_type=F32)
    merged = merged + gate_of(1) * jnp.dot(yb.astype(BF16), wb_ref[1], preferred_element_type=F32)
    merged = merged + gate_of(2) * jnp.dot(yc_ref[...], wb_ref[2], preferred_element_type=F32)
    out = jnp.dot(merged.astype(BF16), wo_ref[...], preferred_element_type=F32) + bo_ref[...]
    gate = mod_ref[:, 2 * D_MODEL:3 * D_MODEL]
    r = DEEPNORM_ALPHA * x_ref[...] + gate * out
    mu = jnp.mean(r, axis=-1, keepdims=True)
    rc = r - mu
    var = jnp.mean(rc * rc, axis=-1, keepdims=True)
    out_ref[...] = rc * lax.rsqrt(var + LN_EPS) * lg_ref[...] + lb_ref[...]


def _merge(y_a, s_f, s_b, p, y_c, x2d, mod, w_branch, w_out, b_out, ssm_g, ln_g, ln_b, tm, mod_row):
    T = x2d.shape[0]
    row = lambda: pl.BlockSpec((tm, D_MODEL), lambda i: (i, 0))
    pcol = lambda off: pl.BlockSpec((tm, D_MODEL), lambda i: (i, off // D_MODEL))
    vec = lambda: pl.BlockSpec((1, D_MODEL), lambda i: (0, 0))
    once = pl.Buffered(1)
    return pl.pallas_call(
        _merge_kernel,
        out_shape=jax.ShapeDtypeStruct((T, D_MODEL), F32),
        grid=(T // tm,),
        in_specs=[row(), row(), row(), pcol(OFF_Z), row(),
                  pl.BlockSpec((tm, 3 * D_MODEL), lambda i: (i, OFF_GM // (3 * D_MODEL))),
                  row(),
                  pl.BlockSpec((None, 1, 3 * D_MODEL), lambda i: (mod_row(i), 0, 0)),
                  pl.BlockSpec((3, D_MODEL, D_MODEL), lambda i: (0, 0, 0), pipeline_mode=once),
                  pl.BlockSpec((D_MODEL, D_MODEL), lambda i: (0, 0), pipeline_mode=once),
                  vec(), vec(), vec(), vec()],
        out_specs=row(),
        compiler_params=_cparams(("arbitrary",)),
        name="merge_out_deepnorm",
    )(y_a, s_f, s_b, p, y_c, p, x2d, mod, w_branch, w_out, b_out, ssm_g, ln_g, ln_b)


def _pack_w_in(w):
    o = 0
    seg = {}
    for name, n in (("u", 1024), ("v", 1024), ("ga", 1024), ("z", 1024), ("xbc", CONV_DIM),
                    ("dt", 2 * SSM_HEADS), ("q", 1024), ("k", KV_WIDTH), ("vv", KV_WIDTH),
                    ("gc", 1024), ("gm", 3 * D_MODEL)):
        seg[name] = w[:, o:o + n]
        o += n
    pad = jnp.zeros((w.shape[0], DT_PAD - 2 * SSM_HEADS), w.dtype)
    packed = jnp.concatenate(
        [seg["u"], seg["v"], seg["ga"], seg["z"], seg["q"], seg["gc"], seg["gm"], seg["xbc"],
         seg["k"], seg["vv"], seg["dt"], pad], axis=1)
    return packed.astype(BF16)


def _rope_tables(n_tokens):
    rows = n_tokens // GRID_W
    row = jnp.broadcast_to(jnp.arange(rows)[:, None], (rows, GRID_W)).reshape(-1)
    col = jnp.broadcast_to(jnp.arange(GRID_W)[None, :], (rows, GRID_W)).reshape(-1)
    inv = ROPE_THETA ** (-jnp.arange(ROPE_PAIRS_PER_AXIS, dtype=F32) / ROPE_PAIRS_PER_AXIS)
    ang = jnp.concatenate([row[:, None] * inv, col[:, None] * inv], axis=-1)
    cos, sin = jnp.cos(ang), jnp.sin(ang)
    cos_f = jnp.stack([cos, cos], axis=-1).reshape(n_tokens, HEAD_DIM)
    sin_f = jnp.stack([-sin, sin], axis=-1).reshape(n_tokens, HEAD_DIM)
    return cos_f, sin_f


def _pad_lanes(v, n):
    return jnp.concatenate([v, jnp.zeros((n - v.shape[0],), v.dtype)])[None, :]


def kernel(x, c, ctx, c_ctx, w_mod, b_mod, w_in, sgu_w, sgu_b, sgu_norm_g, conv_w, conv_b, dt_bias,
           a_log, d_skip, ssm_norm_g, q_norm_g, k_norm_g, w_branch, w_out, b_out, ln_g, ln_b):
    batch, seq, _ = x.shape
    n_ctx = ctx.shape[1]
    assert n_ctx == N_CTX_CHUNKS * CHUNK and seq % 1024 == 0

    cos_l, sin_l = _rope_tables(seq)
    cos_c = jnp.ones((n_ctx, HEAD_DIM), F32)
    sin_c = jnp.zeros((n_ctx, HEAD_DIM), F32)

    c_rows = jnp.concatenate([c, c_ctx[None, :], jnp.zeros((8 - batch - 1, D_MODEL), F32)], axis=0)
    mod_all = _modulation(c_rows, w_mod, b_mod[:, None, :])
    ctx_row = batch

    hid = jnp.arange(CHUNK)[:, None]
    ch = jnp.arange(D_MODEL)[None, :] // SSM_HEAD_DIM
    e_mats = jnp.stack([(hid == ch + d * SSM_HEADS).astype(BF16) for d in range(2)])

    lat_per_tile = 1024
    tiles_per_batch = seq // lat_per_tile
    xl = x.reshape(batch * seq, D_MODEL)
    xc = ctx.reshape(batch * n_ctx, D_MODEL)

    for l in range(DEPTH):
        ctx_out = l < DEPTH - 1
        mod = mod_all[l][:, None, :]
        w_pack = _pack_w_in(w_in[l])
        wb = w_branch[l].astype(BF16)
        wo = w_out[l].astype(BF16)
        q_g, k_g = q_norm_g[l][None, :], k_norm_g[l][None, :]
        sgu_wb = sgu_w[l].astype(BF16)
        sgu_bfull = jnp.repeat(sgu_b[l].T, A_GROUP_DIM, axis=1)
        sgu_g = sgu_norm_g[l][None, :]
        dtb = _pad_lanes(dt_bias[l].reshape(-1), CHUNK)
        alog = _pad_lanes(a_log[l].reshape(-1), CHUNK)

        lat_row = lambda i: i // tiles_per_batch
        p_lat, dt_lat = _inproj(xl, mod, w_pack, lat_per_tile, lat_row)
        p_ctx, dt_ctx = _inproj(xc, mod, w_pack, n_ctx, lambda i: ctx_row)

        xconv_lat = _conv(p_lat, conv_w[l], conv_b[l][None, :], seq, 512)
        xconv_ctx = _conv(p_ctx, conv_w[l], conv_b[l][None, :], n_ctx, n_ctx)
        dskip_x = jnp.repeat(d_skip[l], SSM_HEAD_DIM, axis=1)[:, None, :]
        ssd = _ssd(xconv_ctx, xconv_lat, dt_ctx, dt_lat, dtb, alog, dskip_x, e_mats, batch)

        k_ctx, vt_ctx = _kvprep(p_ctx, cos_c, sin_c, k_g, batch, n_ctx)
        k_lat, vt_lat = _kvprep(p_lat, cos_l, sin_l, k_g, batch, 512)
        o_lat = _attention(p_lat, cos_l, sin_l, q_g, k_ctx, vt_ctx, k_lat, vt_lat, batch, 256, 1024)

        ya_lat = _sgu(p_lat, sgu_wb, sgu_bfull, sgu_g, 512)

        vecs = (b_out[l][None, :], ssm_norm_g[l][None, :], ln_g[l][None, :], ln_b[l][None, :])
        merge_tm = 512
        merge_tiles_per_batch = seq // merge_tm
        xl_new = _merge(ya_lat, ssd[0][1], ssd[1][1], p_lat, o_lat, xl, mod, wb, wo, *vecs,
                        merge_tm, lambda i: i // merge_tiles_per_batch)
        if ctx_out:
            o_ctx = _attention(p_ctx, cos_c, sin_c, q_g, k_ctx, vt_ctx, None, None, batch, n_ctx, 0)
            ya_ctx = _sgu(p_ctx, sgu_wb, sgu_bfull, sgu_g, n_ctx)
            xc = _merge(ya_ctx, ssd[0][0], ssd[1][0], p_ctx, o_ctx, xc, mod, wb, wo, *vecs,
                        n_ctx, lambda i: ctx_row)
        xl = xl_new

    return xl.reshape(batch, seq, D_MODEL)
```

```python
import functools
import math

import jax
import jax.numpy as jnp
import numpy as np
from jax import lax
from jax.experimental import pallas as pl
from jax.experimental.pallas import tpu as pltpu

D_MODEL = 1024
DEPTH = 2
GRID_W = 64
CHUNK = 128
A_GROUPS = 4
A_GROUP_DIM = D_MODEL // A_GROUPS
SSM_HEAD_DIM = 64
SSM_HEADS = 16
SSM_GROUPS = 2
SSM_STATE = 128
D_CONV = 5
CONV_DIM = D_MODEL + 2 * SSM_GROUPS * SSM_STATE
HEAD_DIM = 128
N_Q_HEADS = 8
N_KV_HEADS = 2
GQA_REP = N_Q_HEADS // N_KV_HEADS
KV_WIDTH = N_KV_HEADS * HEAD_DIM
ROPE_THETA = 10000.0
ROPE_PAIRS_PER_AXIS = HEAD_DIM // 4
DEEPNORM_ALPHA = (2 * DEPTH) ** 0.25
LN_EPS = 1e-6
RMS_EPS = 1e-6

OFF_U, OFF_V, OFF_GA, OFF_Z, OFF_Q, OFF_GC, OFF_GM = 0, 1024, 2048, 3072, 4096, 5120, 6144
OFF_XBC = 9216
OFF_K = OFF_XBC + CONV_DIM
OFF_VV = OFF_K + KV_WIDTH
OFF_DT = OFF_VV + KV_WIDTH
DT_PAD = 256
N_PACK = OFF_DT + DT_PAD
PROJ_TN = 1280

BF16_SUBLANES = 16
VT_ROWS = HEAD_DIM + BF16_SUBLANES

VMEM_LIMIT = 56 * 1024 * 1024
F32 = jnp.float32
BF16 = jnp.bfloat16
NEG_BIG = -1e30


def _cparams(sem):
    return pltpu.CompilerParams(dimension_semantics=sem, vmem_limit_bytes=VMEM_LIMIT)


def _silu(t):
    return t * jax.nn.sigmoid(t)


def _split_dot(x, e, terms=3):
    acc = None
    r = x
    for t in range(terms):
        xt = r.astype(BF16)
        if t + 1 < terms:
            r = r - xt.astype(F32)
        part = jnp.dot(xt, e, preferred_element_type=F32)
        acc = part if acc is None else acc + part
    return acc


def _split_dot_rhs(e, x):
    x1 = x.astype(BF16)
    r1 = x - x1.astype(F32)
    x2 = r1.astype(BF16)
    r2 = r1 - x2.astype(F32)
    x3 = r2.astype(BF16)
    acc = jnp.dot(e, x1, preferred_element_type=F32)
    acc = acc + jnp.dot(e, x2, preferred_element_type=F32)
    return acc + jnp.dot(e, x3, preferred_element_type=F32)


def _mod_kernel(c_ref, w_ref, b_ref, o_ref):
    c = c_ref[...]
    act = _silu(c)
    o_ref[...] = jnp.dot(act, w_ref[...], preferred_element_type=F32,
                         precision=lax.Precision.HIGHEST) + b_ref[...]


def _modulation(c_rows, w_mod, b_mod):
    L = w_mod.shape[0]
    tn = 512
    return pl.pallas_call(
        _mod_kernel,
        out_shape=jax.ShapeDtypeStruct((L, 8, 3 * D_MODEL), F32),
        grid=(L, 3 * D_MODEL // tn),
        in_specs=[pl.BlockSpec((8, D_MODEL), lambda l, j: (0, 0)),
                  pl.BlockSpec((None, D_MODEL, tn), lambda l, j: (l, 0, j)),
                  pl.BlockSpec((None, 1, tn), lambda l, j: (l, 0, j))],
        out_specs=pl.BlockSpec((None, 8, tn), lambda l, j: (l, 0, j)),
        compiler_params=_cparams(("arbitrary", "arbitrary")),
        name="adaln_modulation",
    )(c_rows, w_mod, b_mod)


def _inproj_kernel(x_ref, mod_ref, w_ref, o_ref, dt_ref, h_ref):
    j = pl.program_id(1)

    @pl.when(j == 0)
    def _():
        x = x_ref[...]
        mu = jnp.mean(x, axis=-1, keepdims=True)
        xc = x - mu
        var = jnp.mean(xc * xc, axis=-1, keepdims=True)
        shift = mod_ref[:, 0:D_MODEL]
        scale = mod_ref[:, D_MODEL:2 * D_MODEL]
        h = xc * lax.rsqrt(var + LN_EPS) * (1.0 + scale) + shift
        h_ref[...] = h.astype(BF16)

    acc = jnp.dot(h_ref[...], w_ref[...], preferred_element_type=F32)
    o_ref[...] = acc.astype(o_ref.dtype)

    @pl.when(j == pl.num_programs(1) - 1)
    def _():
        dt_ref[...] = acc[:, PROJ_TN - DT_PAD:]


def _inproj(x2d, mod, w_pack, tm, mod_row):
    T = x2d.shape[0]
    assert OFF_DT + DT_PAD == N_PACK and N_PACK % PROJ_TN == 0
    return pl.pallas_call(
        _inproj_kernel,
        out_shape=(jax.ShapeDtypeStruct((T, N_PACK), BF16),
                   jax.ShapeDtypeStruct((T, DT_PAD), F32)),
        grid=(T // tm, N_PACK // PROJ_TN),
        in_specs=[pl.BlockSpec((tm, D_MODEL), lambda i, j: (i, 0)),
                  pl.BlockSpec((None, 1, 3 * D_MODEL), lambda i, j: (mod_row(i), 0, 0)),
                  pl.BlockSpec((D_MODEL, PROJ_TN), lambda i, j: (0, j))],
        out_specs=(pl.BlockSpec((tm, PROJ_TN), lambda i, j: (i, j)),
                   pl.BlockSpec((tm, DT_PAD), lambda i, j: (i, 0))),
        scratch_shapes=[pltpu.VMEM((tm, D_MODEL), BF16)],
        compiler_params=_cparams(("arbitrary", "arbitrary")),
        name="ln_mod_inproj",
    )(x2d, mod, w_pack)


def _sgu_kernel(u_ref, v_ref, ga_ref, w_ref, b_ref, g_ref, o_ref, *, n_chunks):
    for c in range(n_chunks):
        rows = slice(c * CHUNK, (c + 1) * CHUNK)
        for g in range(A_GROUPS):
            cols = slice(g * A_GROUP_DIM, (g + 1) * A_GROUP_DIM)
            v = v_ref[rows, cols].astype(F32)
            mu = jnp.mean(v, axis=-1, keepdims=True)
            vc = v - mu
            var = jnp.mean(vc * vc, axis=-1, keepdims=True)
            vn = vc * lax.rsqrt(var + LN_EPS) * g_ref[:, cols]
            mixed = jnp.dot(w_ref[g], vn.astype(BF16), preferred_element_type=F32)
            mixed = mixed + b_ref[:, cols]
            y = u_ref[rows, cols].astype(F32) * mixed * _silu(ga_ref[rows, cols].astype(F32))
            o_ref[rows, cols] = y.astype(o_ref.dtype)


CONV_HALO = 16


def _conv_kernel(prev_ref, cur_ref, next_ref, w_ref, b_ref, o_ref, ext_ref, *, tiles_per_seq, tc):
    i = pl.program_id(0)
    pos = i % tiles_per_seq
    zeros = jnp.zeros((CONV_HALO, CONV_DIM), F32)
    ext_ref[0:CONV_HALO, :] = jnp.where(pos == 0, zeros, prev_ref[...].astype(F32))
    ext_ref[CONV_HALO:CONV_HALO + tc, :] = cur_ref[...].astype(F32)
    ext_ref[CONV_HALO + tc:, :] = jnp.where(pos == tiles_per_seq - 1, zeros, next_ref[...].astype(F32))
    pad = (D_CONV - 1) // 2
    acc = jnp.zeros((tc, CONV_DIM), F32) + b_ref[...]
    for j in range(D_CONV):
        start = CONV_HALO - pad + j
        acc = acc + ext_ref[start:start + tc, :] * w_ref[j:j + 1, :]
    o_ref[...] = _silu(acc).astype(o_ref.dtype)


def _conv(p, conv_w, conv_b, seq_len, tc):
    T = p.shape[0]
    n_tiles = T // tc
    hb = tc // CONV_HALO
    n_hblocks = T // CONV_HALO
    colb = OFF_XBC // CONV_DIM
    return pl.pallas_call(
        functools.partial(_conv_kernel, tiles_per_seq=seq_len // tc, tc=tc),
        out_shape=jax.ShapeDtypeStruct((T, CONV_DIM), BF16),
        grid=(n_tiles,),
        in_specs=[pl.BlockSpec((CONV_HALO, CONV_DIM), lambda i: (jnp.maximum(i * hb - 1, 0), colb)),
                  pl.BlockSpec((tc, CONV_DIM), lambda i: (i, colb)),
                  pl.BlockSpec((CONV_HALO, CONV_DIM),
                               lambda i: (jnp.minimum((i + 1) * hb, n_hblocks - 1), colb)),
                  pl.BlockSpec((D_CONV, CONV_DIM), lambda i: (0, 0)),
                  pl.BlockSpec((1, CONV_DIM), lambda i: (0, 0))],
        out_specs=pl.BlockSpec((tc, CONV_DIM), lambda i: (i, 0)),
        scratch_shapes=[pltpu.VMEM((tc + 2 * CONV_HALO, CONV_DIM), F32)],
        compiler_params=_cparams(("arbitrary",)),
        name="dwconv_silu",
    )(p, p, p, conv_w, conv_b)


N_CTX_CHUNKS = 2


SSD_BLOCK = 512


def _ssd_chunk(x_cols, dt_raw, dtb, alog, dskip_ref, e_ref, ht_ref, y_ref, d):
    reverse = d == 1
    dt = jax.nn.softplus(dt_raw + dtb)
    a = dt * (-jnp.exp(alog))

    row = lax.broadcasted_iota(jnp.int32, (CHUNK, CHUNK), 0)
    col = lax.broadcasted_iota(jnp.int32, (CHUNK, CHUNK), 1)
    mask = (row <= col) if reverse else (row >= col)
    tri = mask.astype(BF16)
    tri_t = ((row >= col) if reverse else (row <= col)).astype(BF16)

    a_cs = _split_dot_rhs(tri, a)
    a_cs_t = _split_dot(a.T, tri_t)
    end = 0 if reverse else CHUNK - 1

    lane = lax.broadcasted_iota(jnp.int32, (CHUNK, 2 * SSM_HEAD_DIM), 1)
    left = lane < SSM_HEAD_DIM
    gw = D_MODEL // SSM_GROUPS
    heads_per_group = SSM_HEADS // SSM_GROUPS
    heads_per_block = SSD_BLOCK // SSM_HEAD_DIM
    for g in range(SSM_GROUPS):
        b_g = x_cols(D_MODEL + g * SSM_STATE, D_MODEL + (g + 1) * SSM_STATE)
        c_g = x_cols(D_MODEL + (SSM_GROUPS + g) * SSM_STATE, D_MODEL + (SSM_GROUPS + g + 1) * SSM_STATE)
        cb = lax.dot_general(c_g, b_g, (((1,), (1,)), ((), ())), preferred_element_type=F32)
        b_t = b_g.astype(F32).T.astype(BF16)
        for blk in range(gw // SSD_BLOCK):
            c0 = g * gw + blk * SSD_BLOCK
            cols = slice(c0, c0 + SSD_BLOCK)
            gcols = slice(blk * SSD_BLOCK, (blk + 1) * SSD_BLOCK)
            e_blk = e_ref[:, cols]
            a_cs_x = _split_dot(a_cs, e_blk, terms=2)
            dt_x = _split_dot(dt, e_blk, terms=1)
            a_tot_x = a_cs_x[end:end + 1, :]
            xs = x_cols(c0, c0 + SSD_BLOCK).astype(F32)
            xdt = xs * dt_x
            xdt_bf = xdt.astype(BF16)
            xw = (xdt * jnp.exp(a_tot_x - a_cs_x)).astype(BF16)
            h_in = ht_ref[g, :, gcols]
            y_blk = jnp.dot(c_g, h_in.astype(BF16), preferred_element_type=F32) * jnp.exp(a_cs_x)
            ht_ref[g, :, gcols] = h_in * jnp.exp(a_tot_x) + jnp.dot(b_t, xw, preferred_element_type=F32)
            y_pairs = []
            for j in range(heads_per_block // 2):
                ms = []
                for k in range(2):
                    hh = d * SSM_HEADS + g * heads_per_group + blk * heads_per_block + 2 * j + k
                    seg = a_cs[:, hh:hh + 1] - a_cs_t[hh:hh + 1, :]
                    dec = jnp.exp(jnp.where(mask, seg, NEG_BIG))
                    ms.append((cb * dec).astype(BF16))
                lhs = jnp.concatenate(ms, axis=1)
                xp = xdt_bf[:, 2 * j * SSM_HEAD_DIM:(2 * j + 2) * SSM_HEAD_DIM]
                zero = jnp.zeros_like(xp)
                rhs = jnp.concatenate([jnp.where(left, xp, zero), jnp.where(left, zero, xp)], axis=0)
                y_pairs.append(jnp.dot(lhs, rhs, preferred_element_type=F32))
            y_blk = y_blk + jnp.concatenate(y_pairs, axis=1) + dskip_ref[:, cols] * xs
            y_ref[:, cols] = y_blk.astype(y_ref.dtype)


def _ssd_kernel(xcf_ref, xlf_ref, dtcf_ref, dtlf_ref, xcr_ref, xlr_ref, dtcr_ref, dtlr_ref,
                dtb_ref, alog_ref, dskip_ref, e_ref,
                ycf_ref, ylf_ref, ycr_ref, ylr_ref, ht_ref, y_ref):
    s = pl.program_id(0)
    is_ctx = s < N_CTX_CHUNKS
    batch = xcf_ref.shape[0]

    @pl.when(s == 0)
    def _():
        ht_ref[...] = jnp.zeros_like(ht_ref)

    sides = ((xcf_ref, xlf_ref, dtcf_ref, dtlf_ref, ycf_ref, ylf_ref),
             (xcr_ref, xlr_ref, dtcr_ref, dtlr_ref, ycr_ref, ylr_ref))
    for b in range(batch):
        for d, (xc_ref, xl_ref, dtc_ref, dtl_ref, _, _) in enumerate(sides):
            x_cols = lambda c0, c1, b=b, xc_ref=xc_ref, xl_ref=xl_ref: jnp.where(
                is_ctx, xc_ref[b, :, c0:c1], xl_ref[b, :, c0:c1])
            dt_raw = jnp.where(is_ctx, dtc_ref[b, :, 0:CHUNK], dtl_ref[b, :, 0:CHUNK])
            _ssd_chunk(x_cols, dt_raw, dtb_ref[...], alog_ref[...], dskip_ref.at[d], e_ref.at[d],
                       ht_ref.at[b, d], y_ref.at[b, d], d)

    @pl.when(is_ctx)
    def _():
        for d in range(2):
            sides[d][4][...] = y_ref[:, d]

    @pl.when(jnp.logical_not(is_ctx))
    def _():
        for d in range(2):
            sides[d][5][...] = y_ref[:, d]


def _ssd(xconv_ctx, xconv_lat, dt_ctx, dt_lat, dtb, alog, dskip_x, e_mats, batch):
    n_lat = xconv_lat.shape[0] // batch // CHUNK
    n_steps = N_CTX_CHUNKS + n_lat
    per_batch = lambda t: t.reshape(batch, t.shape[0] // batch, t.shape[1])

    cidx = (lambda s: jnp.minimum(s, N_CTX_CHUNKS - 1),
            lambda s: jnp.maximum(N_CTX_CHUNKS - 1 - s, 0))
    lidx = (lambda s: jnp.maximum(s - N_CTX_CHUNKS, 0),
            lambda s: jnp.minimum(n_steps - 1 - s, n_lat - 1))

    def specs(width, d):
        return [pl.BlockSpec((batch, CHUNK, width), lambda s, d=d: (0, cidx[d](s), 0)),
                pl.BlockSpec((batch, CHUNK, width), lambda s, d=d: (0, lidx[d](s), 0))]

    const = lambda shape: pl.BlockSpec(shape, lambda s: (0,) * len(shape))
    in_specs, out_specs, args = [], [], []
    for d in range(2):
        in_specs += specs(CONV_DIM, d) + specs(DT_PAD, d)
        args += [per_batch(xconv_ctx), per_batch(xconv_lat), per_batch(dt_ctx), per_batch(dt_lat)]
        out_specs += specs(D_MODEL, d)
    in_specs += [const((1, CHUNK)), const((1, CHUNK)), const((2, 1, D_MODEL)), const((2, CHUNK, D_MODEL))]
    y_shapes = (jax.ShapeDtypeStruct((batch, xconv_ctx.shape[0] // batch, D_MODEL), BF16),
                jax.ShapeDtypeStruct((batch, xconv_lat.shape[0] // batch, D_MODEL), BF16))
    ys = pl.pallas_call(
        _ssd_kernel,
        out_shape=y_shapes + y_shapes,
        grid=(n_steps,),
        in_specs=in_specs,
        out_specs=tuple(out_specs),
        scratch_shapes=[pltpu.VMEM((batch, 2, SSM_GROUPS, SSM_STATE, D_MODEL // SSM_GROUPS), F32),
                        pltpu.VMEM((batch, 2, CHUNK, D_MODEL), BF16)],
        compiler_params=_cparams(("arbitrary",)),
        name="ssd_scan_bidir",
    )(*args, dtb, alog, dskip_x, e_mats)
    ycf, ylf, ycr, ylr = (y.reshape(-1, D_MODEL) for y in ys)
    return (ycf, ylf), (ycr, ylr)


def _norm_rope(t, g, cos, sin):
    tn = t * lax.rsqrt(jnp.mean(t * t, axis=-1, keepdims=True) + RMS_EPS) * g
    lane = lax.broadcasted_iota(jnp.int32, tn.shape, 1)
    partner = jnp.where(lane % 2 == 0, pltpu.roll(tn, HEAD_DIM - 1, axis=1), pltpu.roll(tn, 1, axis=1))
    return tn * cos + partner * sin


def _kvprep_kernel(k_ref, v_ref, cos_ref, sin_ref, g_ref, ko_ref, vto_ref):
    for h in range(N_KV_HEADS):
        cols = slice(h * HEAD_DIM, (h + 1) * HEAD_DIM)
        kh = _norm_rope(k_ref[:, cols].astype(F32), g_ref[...], cos_ref[...], sin_ref[...])
        ko_ref[h] = kh.astype(BF16)
        vto_ref[h, 0:HEAD_DIM, :] = v_ref[:, cols].astype(F32).T.astype(BF16)
        vto_ref[h, HEAD_DIM:, :] = jnp.ones((VT_ROWS - HEAD_DIM, k_ref.shape[0]), BF16)


def _kvprep(p, cos, sin, k_g, batch, tm):
    T = p.shape[0]
    S = T // batch
    nt = S // tm
    return pl.pallas_call(
        _kvprep_kernel,
        out_shape=(jax.ShapeDtypeStruct((batch, N_KV_HEADS, S, HEAD_DIM), BF16),
                   jax.ShapeDtypeStruct((batch, N_KV_HEADS, VT_ROWS, S), BF16)),
        grid=(batch, nt),
        in_specs=[pl.BlockSpec((tm, KV_WIDTH), lambda b, i: (b * nt + i, OFF_K // KV_WIDTH)),
                  pl.BlockSpec((tm, KV_WIDTH), lambda b, i: (b * nt + i, OFF_VV // KV_WIDTH)),
                  pl.BlockSpec((tm, HEAD_DIM), lambda b, i: (i, 0)),
                  pl.BlockSpec((tm, HEAD_DIM), lambda b, i: (i, 0)),
                  pl.BlockSpec((1, HEAD_DIM), lambda b, i: (0, 0))],
        out_specs=(pl.BlockSpec((None, N_KV_HEADS, tm, HEAD_DIM), lambda b, i: (b, 0, i, 0)),
                   pl.BlockSpec((None, N_KV_HEADS, VT_ROWS, tm), lambda b, i: (b, 0, 0, i))),
        compiler_params=_cparams(("arbitrary", "arbitrary")),
        name="kv_prep",
    )(p, p, cos, sin, k_g)


LOG2E = 1.4426950408889634


def _attn_kernel(*refs, tq, tk, n_lat_chunks):
    if n_lat_chunks:
        (q_ref, gc_ref, cos_ref, sin_ref, g_ref, kc_ref, vtc_ref, kl_ref, vtl_ref, o_ref,
         qt_ref, acc_ref, s0_ref, s1_ref) = refs
    else:
        q_ref, gc_ref, cos_ref, sin_ref, g_ref, kc_ref, vtc_ref, o_ref, qt_ref, acc_ref = refs
    scale = HEAD_DIM ** -0.5 * LOG2E
    for h in range(GQA_REP):
        cols = slice(h * HEAD_DIM, (h + 1) * HEAD_DIM)
        qh = _norm_rope(q_ref[:, cols].astype(F32), g_ref[...], cos_ref[...], sin_ref[...]) * scale
        qt_ref[:, h * tq:(h + 1) * tq] = qh.T.astype(BF16)

    def scores(k):
        return jnp.dot(k, qt_ref[...], preferred_element_type=F32)

    st = scores(kc_ref[...])
    m = jnp.max(st, axis=0, keepdims=True)
    p = jnp.exp2(st - m)
    acc_ref[...] = jnp.dot(vtc_ref[...], p.astype(BF16), preferred_element_type=F32)

    if n_lat_chunks:
        def offset(j):
            return j * tk if isinstance(j, int) else pl.multiple_of(j * tk, tk)

        def k_chunk(j):
            return kl_ref[pl.ds(offset(j), tk), :]

        def vt_chunk(j):
            return vtl_ref[:, pl.ds(offset(j), tk)]

        def scores_into(s_ref, k):
            st = scores(k)
            s_ref[...] = st
            return jnp.max(st, axis=0, keepdims=True)

        def update(s_ref, vt, m, cmax):
            m_new = jnp.maximum(m, cmax)
            alpha = jnp.exp2(m - m_new)
            p = jnp.exp2(s_ref[...] - m_new)
            acc_ref[...] = alpha * acc_ref[...] + jnp.dot(vt, p.astype(BF16), preferred_element_type=F32)
            return m_new

        cm0 = scores_into(s0_ref, k_chunk(0))

        def pair(jj, carry):
            m, cm0 = carry
            j = 2 * jj
            cm1 = scores_into(s1_ref, k_chunk(j + 1))
            m = update(s0_ref, vt_chunk(j), m, cm0)
            cm0 = scores_into(s0_ref, k_chunk(j + 2))
            m = update(s1_ref, vt_chunk(j + 1), m, cm1)
            return m, cm0

        m, cm0 = lax.fori_loop(0, n_lat_chunks // 2 - 1, pair, (m, cm0))
        j = n_lat_chunks - 2
        cm1 = scores_into(s1_ref, k_chunk(j + 1))
        m = update(s0_ref, vt_chunk(j), m, cm0)
        m = update(s1_ref, vt_chunk(j + 1), m, cm1)

    inv = 1.0 / acc_ref[HEAD_DIM:HEAD_DIM + 1, :]
    for h in range(GQA_REP):
        cols = slice(h * HEAD_DIM, (h + 1) * HEAD_DIM)
        lanes = slice(h * tq, (h + 1) * tq)
        oh = (acc_ref[0:HEAD_DIM, lanes] * inv[:, lanes]).T
        o_ref[:, cols] = (oh * _silu(gc_ref[:, cols].astype(F32))).astype(o_ref.dtype)


def _attention(p_q, cos, sin, q_g, k_ctx, vt_ctx, k_lat, vt_lat, batch, tq, tk):
    T = p_q.shape[0]
    S = T // batch
    nq = S // tq
    n_ctx = k_ctx.shape[2]
    n_lat_chunks = 0 if k_lat is None else k_lat.shape[2] // tk
    assert n_lat_chunks % 2 == 0
    gw = GQA_REP * HEAD_DIM
    in_specs = [pl.BlockSpec((tq, gw), lambda b, g, i: (b * nq + i, OFF_Q // gw + g)),
                pl.BlockSpec((tq, gw), lambda b, g, i: (b * nq + i, OFF_GC // gw + g)),
                pl.BlockSpec((tq, HEAD_DIM), lambda b, g, i: (i, 0)),
                pl.BlockSpec((tq, HEAD_DIM), lambda b, g, i: (i, 0)),
                pl.BlockSpec((1, HEAD_DIM), lambda b, g, i: (0, 0)),
                pl.BlockSpec((None, None, n_ctx, HEAD_DIM), lambda b, g, i: (b, g, 0, 0)),
                pl.BlockSpec((None, None, VT_ROWS, n_ctx), lambda b, g, i: (b, g, 0, 0))]
    args = [p_q, p_q, cos, sin, q_g, k_ctx, vt_ctx]
    scratch = [pltpu.VMEM((HEAD_DIM, GQA_REP * tq), BF16), pltpu.VMEM((VT_ROWS, GQA_REP * tq), F32)]
    if n_lat_chunks:
        n_lat = k_lat.shape[2]
        in_specs += [pl.BlockSpec((None, None, n_lat, HEAD_DIM), lambda b, g, i: (b, g, 0, 0)),
                     pl.BlockSpec((None, None, VT_ROWS, n_lat), lambda b, g, i: (b, g, 0, 0))]
        args += [k_lat, vt_lat]
        scratch += [pltpu.VMEM((tk, GQA_REP * tq), F32), pltpu.VMEM((tk, GQA_REP * tq), F32)]
    return pl.pallas_call(
        functools.partial(_attn_kernel, tq=tq, tk=tk, n_lat_chunks=n_lat_chunks),
        out_shape=jax.ShapeDtypeStruct((T, D_MODEL), BF16),
        grid=(batch, N_KV_HEADS, nq),
        in_specs=in_specs,
        out_specs=pl.BlockSpec((tq, gw), lambda b, g, i: (b * nq + i, g)),
        scratch_shapes=scratch,
        compiler_params=_cparams(("arbitrary", "arbitrary", "arbitrary")),
        name="gqa_flash_lat" if n_lat_chunks else "gqa_flash_ctx",
    )(*args)


def _merge_kernel(u_ref, v_ref, ga_ref, sw_ref, sbias_ref, sgain_ref, sf_ref, sb_ref, z_ref, yc_ref, gm_ref,
                  x_ref, mod_ref, wb_ref, wo_ref, bo_ref, sg_ref, lg_ref, lb_ref, out_ref, ya_ref):
    _sgu_kernel(u_ref, v_ref, ga_ref, sw_ref, sbias_ref, sgain_ref, ya_ref, n_chunks=u_ref.shape[0] // CHUNK)
    s = (sf_ref[...].astype(F32) + sb_ref[...].astype(F32)) * _silu(z_ref[...].astype(F32))
    yb = s * lax.rsqrt(jnp.mean(s * s, axis=-1, keepdims=True) + RMS_EPS) * sg_ref[...]
    gate_of = lambda k: jax.nn.sigmoid(gm_ref[:, k * D_MODEL:(k + 1) * D_MODEL].astype(F32))
    merged = gate_of(0) * jnp.dot(ya_ref[...], wb_ref[0], preferred_element_type=F32)
    merged = merged + gate_of(1) * jnp.dot(yb.astype(BF16), wb_ref[1], preferred_element_type=F32)
    merged = merged + gate_of(2) * jnp.dot(yc_ref[...], wb_ref[2], preferred_element_type=F32)
    out = jnp.dot(merged.astype(BF16), wo_ref[...], preferred_element_type=F32) + bo_ref[...]
    gate = mod_ref[:, 2 * D_MODEL:3 * D_MODEL]
    r = DEEPNORM_ALPHA * x_ref[...] + gate * out
    mu = jnp.mean(r, axis=-1, keepdims=True)
    rc = r - mu
    var = jnp.mean(rc * rc, axis=-1, keepdims=True)
    out_ref[...] = rc * lax.rsqrt(var + LN_EPS) * lg_ref[...] + lb_ref[...]


def _merge(sgu_w, sgu_bfull, sgu_g, s_f, s_b, p, y_c, x2d, mod, w_branch, w_out, b_out, ssm_g, ln_g, ln_b,
           tm, mod_row):
    T = x2d.shape[0]
    row = lambda: pl.BlockSpec((tm, D_MODEL), lambda i: (i, 0))
    pcol = lambda off: pl.BlockSpec((tm, D_MODEL), lambda i: (i, off // D_MODEL))
    vec = lambda: pl.BlockSpec((1, D_MODEL), lambda i: (0, 0))
    once = pl.Buffered(1)
    return pl.pallas_call(
        _merge_kernel,
        out_shape=jax.ShapeDtypeStruct((T, D_MODEL), F32),
        grid=(T // tm,),
        in_specs=[pcol(OFF_U), pcol(OFF_V), pcol(OFF_GA),
                  pl.BlockSpec((A_GROUPS, CHUNK, CHUNK), lambda i: (0, 0, 0)),
                  pl.BlockSpec((CHUNK, D_MODEL), lambda i: (0, 0)),
                  vec(),
                  row(), row(), pcol(OFF_Z), row(),
                  pl.BlockSpec((tm, 3 * D_MODEL), lambda i: (i, OFF_GM // (3 * D_MODEL))),
                  row(),
                  pl.BlockSpec((None, 1, 3 * D_MODEL), lambda i: (mod_row(i), 0, 0)),
                  pl.BlockSpec((3, D_MODEL, D_MODEL), lambda i: (0, 0, 0), pipeline_mode=once),
                  pl.BlockSpec((D_MODEL, D_MODEL), lambda i: (0, 0), pipeline_mode=once),
                  vec(), vec(), vec(), vec()],
        out_specs=row(),
        scratch_shapes=[pltpu.VMEM((tm, D_MODEL), BF16)],
        compiler_params=_cparams(("arbitrary",)),
        name="sgu_merge_out_deepnorm",
    )(p, p, p, sgu_w, sgu_bfull, sgu_g, s_f, s_b, p, y_c, p, x2d, mod, w_branch, w_out, b_out, ssm_g, ln_g, ln_b)


def _pack_w_in(w):
    o = 0
    seg = {}
    for name, n in (("u", 1024), ("v", 1024), ("ga", 1024), ("z", 1024), ("xbc", CONV_DIM),
                    ("dt", 2 * SSM_HEADS), ("q", 1024), ("k", KV_WIDTH), ("vv", KV_WIDTH),
                    ("gc", 1024), ("gm", 3 * D_MODEL)):
        seg[name] = w[:, o:o + n]
        o += n
    pad = jnp.zeros((w.shape[0], DT_PAD - 2 * SSM_HEADS), w.dtype)
    packed = jnp.concatenate(
        [seg["u"], seg["v"], seg["ga"], seg["z"], seg["q"], seg["gc"], seg["gm"], seg["xbc"],
         seg["k"], seg["vv"], seg["dt"], pad], axis=1)
    return packed.astype(BF16)


def _rope_tables(n_tokens):
    rows = n_tokens // GRID_W
    row = np.repeat(np.arange(rows, dtype=np.float32), GRID_W)
    col = np.tile(np.arange(GRID_W, dtype=np.float32), rows)
    inv = np.float32(ROPE_THETA) ** (-np.arange(ROPE_PAIRS_PER_AXIS, dtype=np.float32) / ROPE_PAIRS_PER_AXIS)
    ang = np.concatenate([row[:, None] * inv, col[:, None] * inv], axis=-1).astype(np.float32)
    cos, sin = np.cos(ang), np.sin(ang)
    cos_f = np.stack([cos, cos], axis=-1).reshape(n_tokens, HEAD_DIM)
    sin_f = np.stack([-sin, sin], axis=-1).reshape(n_tokens, HEAD_DIM)
    return jnp.asarray(cos_f, F32), jnp.asarray(sin_f, F32)


def _pad_lanes(v, n):
    return jnp.concatenate([v, jnp.zeros((n - v.shape[0],), v.dtype)])[None, :]


def kernel(x, c, ctx, c_ctx, w_mod, b_mod, w_in, sgu_w, sgu_b, sgu_norm_g, conv_w, conv_b, dt_bias,
           a_log, d_skip, ssm_norm_g, q_norm_g, k_norm_g, w_branch, w_out, b_out, ln_g, ln_b):
    batch, seq, _ = x.shape
    n_ctx = ctx.shape[1]
    assert n_ctx == N_CTX_CHUNKS * CHUNK and seq % 1024 == 0

    cos_l, sin_l = _rope_tables(seq)
    cos_c = jnp.ones((n_ctx, HEAD_DIM), F32)
    sin_c = jnp.zeros((n_ctx, HEAD_DIM), F32)

    c_rows = jnp.concatenate([c, c_ctx[None, :], jnp.zeros((8 - batch - 1, D_MODEL), F32)], axis=0)
    mod_all = _modulation(c_rows, w_mod, b_mod[:, None, :])
    ctx_row = batch

    hid = jnp.arange(CHUNK)[:, None]
    ch = jnp.arange(D_MODEL)[None, :] // SSM_HEAD_DIM
    e_mats = jnp.stack([(hid == ch + d * SSM_HEADS).astype(BF16) for d in range(2)])

    lat_per_tile = 1024
    tiles_per_batch = seq // lat_per_tile
    xl = x.reshape(batch * seq, D_MODEL)
    xc = ctx.reshape(batch * n_ctx, D_MODEL)

    for l in range(DEPTH):
        ctx_out = l < DEPTH - 1
        mod = mod_all[l][:, None, :]
        w_pack = _pack_w_in(w_in[l])
        wb = w_branch[l].astype(BF16)
        wo = w_out[l].astype(BF16)
        q_g, k_g = q_norm_g[l][None, :], k_norm_g[l][None, :]
        sgu_wb = sgu_w[l].astype(BF16)
        sgu_bfull = jnp.repeat(sgu_b[l].T, A_GROUP_DIM, axis=1)
        sgu_g = sgu_norm_g[l][None, :]
        dtb = _pad_lanes(dt_bias[l].reshape(-1), CHUNK)
        alog = _pad_lanes(a_log[l].reshape(-1), CHUNK)

        lat_row = lambda i: i // tiles_per_batch
        p_lat, dt_lat = _inproj(xl, mod, w_pack, lat_per_tile, lat_row)
        p_ctx, dt_ctx = _inproj(xc, mod, w_pack, n_ctx, lambda i: ctx_row)

        xconv_lat = _conv(p_lat, conv_w[l], conv_b[l][None, :], seq, 512)
        xconv_ctx = _conv(p_ctx, conv_w[l], conv_b[l][None, :], n_ctx, n_ctx)
        dskip_x = jnp.repeat(d_skip[l], SSM_HEAD_DIM, axis=1)[:, None, :]
        ssd = _ssd(xconv_ctx, xconv_lat, dt_ctx, dt_lat, dtb, alog, dskip_x, e_mats, batch)

        k_ctx, vt_ctx = _kvprep(p_ctx, cos_c, sin_c, k_g, batch, n_ctx)
        k_lat, vt_lat = _kvprep(p_lat, cos_l, sin_l, k_g, batch, 512)
        o_lat = _attention(p_lat, cos_l, sin_l, q_g, k_ctx, vt_ctx, k_lat, vt_lat, batch, 256, 1024)

        sgu = (sgu_wb, sgu_bfull, sgu_g)
        vecs = (b_out[l][None, :], ssm_norm_g[l][None, :], ln_g[l][None, :], ln_b[l][None, :])
        merge_tm = 512
        merge_tiles_per_batch = seq // merge_tm
        xl_new = _merge(*sgu, ssd[0][1], ssd[1][1], p_lat, o_lat, xl, mod, wb, wo, *vecs,
                        merge_tm, lambda i: i // merge_tiles_per_batch)
        if ctx_out:
            o_ctx = _attention(p_ctx, cos_c, sin_c, q_g, k_ctx, vt_ctx, None, None, batch, n_ctx, 0)
            xc = _merge(*sgu, ssd[0][0], ssd[1][0], p_ctx, o_ctx, xc, mod, wb, wo, *vecs,
                        n_ctx, lambda i: ctx_row)
        xl = xl_new

    return xl.reshape(batch, seq, D_MODEL)
```

```python
import functools
import math

import jax
import jax.numpy as jnp
import numpy as np
from jax import lax
from jax.experimental import pallas as pl
from jax.experimental.pallas import tpu as pltpu

D_MODEL = 1024
DEPTH = 2
GRID_W = 64
CHUNK = 128
A_GROUPS = 4
A_GROUP_DIM = D_MODEL // A_GROUPS
SSM_HEAD_DIM = 64
SSM_HEADS = 16
SSM_GROUPS = 2
SSM_STATE = 128
D_CONV = 5
CONV_DIM = D_MODEL + 2 * SSM_GROUPS * SSM_STATE
HEAD_DIM = 128
N_Q_HEADS = 8
N_KV_HEADS = 2
GQA_REP = N_Q_HEADS // N_KV_HEADS
KV_WIDTH = N_KV_HEADS * HEAD_DIM
ROPE_THETA = 10000.0
ROPE_PAIRS_PER_AXIS = HEAD_DIM // 4
DEEPNORM_ALPHA = (2 * DEPTH) ** 0.25
LN_EPS = 1e-6
RMS_EPS = 1e-6

OFF_U, OFF_V, OFF_GA, OFF_Z, OFF_Q, OFF_GC, OFF_GM = 0, 1024, 2048, 3072, 4096, 5120, 6144
OFF_XBC = 9216
OFF_K = OFF_XBC + CONV_DIM
OFF_VV = OFF_K + KV_WIDTH
OFF_DT = OFF_VV + KV_WIDTH
DT_PAD = 256
N_PACK = OFF_DT + DT_PAD
PROJ_TN = 1280

BF16_SUBLANES = 16
VT_ROWS = HEAD_DIM + BF16_SUBLANES

VMEM_LIMIT = 56 * 1024 * 1024
F32 = jnp.float32
BF16 = jnp.bfloat16
NEG_BIG = -1e30


def _cparams(sem):
    return pltpu.CompilerParams(dimension_semantics=sem, vmem_limit_bytes=VMEM_LIMIT)


def _silu(t):
    return t * jax.nn.sigmoid(t)


def _split_dot(x, e, terms=3):
    acc = None
    r = x
    for t in range(terms):
        xt = r.astype(BF16)
        if t + 1 < terms:
            r = r - xt.astype(F32)
        part = jnp.dot(xt, e, preferred_element_type=F32)
        acc = part if acc is None else acc + part
    return acc


def _split_dot_rhs(e, x):
    x1 = x.astype(BF16)
    r1 = x - x1.astype(F32)
    x2 = r1.astype(BF16)
    r2 = r1 - x2.astype(F32)
    x3 = r2.astype(BF16)
    acc = jnp.dot(e, x1, preferred_element_type=F32)
    acc = acc + jnp.dot(e, x2, preferred_element_type=F32)
    return acc + jnp.dot(e, x3, preferred_element_type=F32)


def _mod_kernel(c_ref, w_ref, b_ref, o_ref):
    c = c_ref[...]
    act = _silu(c)
    o_ref[...] = jnp.dot(act, w_ref[...], preferred_element_type=F32,
                         precision=lax.Precision.HIGHEST) + b_ref[...]


def _modulation(c_rows, w_mod, b_mod):
    L = w_mod.shape[0]
    tn = 512
    return pl.pallas_call(
        _mod_kernel,
        out_shape=jax.ShapeDtypeStruct((L, 8, 3 * D_MODEL), F32),
        grid=(L, 3 * D_MODEL // tn),
        in_specs=[pl.BlockSpec((8, D_MODEL), lambda l, j: (0, 0)),
                  pl.BlockSpec((None, D_MODEL, tn), lambda l, j: (l, 0, j)),
                  pl.BlockSpec((None, 1, tn), lambda l, j: (l, 0, j))],
        out_specs=pl.BlockSpec((None, 8, tn), lambda l, j: (l, 0, j)),
        compiler_params=_cparams(("arbitrary", "arbitrary")),
        name="adaln_modulation",
    )(c_rows, w_mod, b_mod)


def _inproj_kernel(x_ref, mod_ref, w_ref, o_ref, dt_ref, h_ref):
    j = pl.program_id(1)

    @pl.when(j == 0)
    def _():
        x = x_ref[...]
        mu = jnp.mean(x, axis=-1, keepdims=True)
        xc = x - mu
        var = jnp.mean(xc * xc, axis=-1, keepdims=True)
        shift = mod_ref[:, 0:D_MODEL]
        scale = mod_ref[:, D_MODEL:2 * D_MODEL]
        h = xc * lax.rsqrt(var + LN_EPS) * (1.0 + scale) + shift
        h_ref[...] = h.astype(BF16)

    acc = jnp.dot(h_ref[...], w_ref[...], preferred_element_type=F32)
    o_ref[...] = acc.astype(o_ref.dtype)

    @pl.when(j == pl.num_programs(1) - 1)
    def _():
        dt_ref[...] = acc[:, PROJ_TN - DT_PAD:]


def _inproj(x2d, mod, w_pack, layer, tm, mod_row):
    T = x2d.shape[0]
    assert OFF_DT + DT_PAD == N_PACK and N_PACK % PROJ_TN == 0
    return pl.pallas_call(
        _inproj_kernel,
        out_shape=(jax.ShapeDtypeStruct((T, N_PACK), BF16),
                   jax.ShapeDtypeStruct((T, DT_PAD), F32)),
        grid=(T // tm, N_PACK // PROJ_TN),
        in_specs=[pl.BlockSpec((tm, D_MODEL), lambda i, j: (i, 0)),
                  pl.BlockSpec((None, 1, 3 * D_MODEL), lambda i, j: (mod_row(i), 0, 0)),
                  pl.BlockSpec((None, D_MODEL, PROJ_TN), lambda i, j: (layer, 0, j))],
        out_specs=(pl.BlockSpec((tm, PROJ_TN), lambda i, j: (i, j)),
                   pl.BlockSpec((tm, DT_PAD), lambda i, j: (i, 0))),
        scratch_shapes=[pltpu.VMEM((tm, D_MODEL), BF16)],
        compiler_params=_cparams(("arbitrary", "arbitrary")),
        name="ln_mod_inproj",
    )(x2d, mod, w_pack)


def _sgu_kernel(u_ref, v_ref, ga_ref, w_ref, b_ref, g_ref, o_ref, *, n_chunks):
    for c in range(n_chunks):
        rows = slice(c * CHUNK, (c + 1) * CHUNK)
        for g in range(A_GROUPS):
            cols = slice(g * A_GROUP_DIM, (g + 1) * A_GROUP_DIM)
            v = v_ref[rows, cols].astype(F32)
            mu = jnp.mean(v, axis=-1, keepdims=True)
            vc = v - mu
            var = jnp.mean(vc * vc, axis=-1, keepdims=True)
            vn = vc * lax.rsqrt(var + LN_EPS) * g_ref[:, cols]
            mixed = jnp.dot(w_ref[g], vn.astype(BF16), preferred_element_type=F32)
            mixed = mixed + b_ref[:, cols]
            y = u_ref[rows, cols].astype(F32) * mixed * _silu(ga_ref[rows, cols].astype(F32))
            o_ref[rows, cols] = y.astype(o_ref.dtype)


CONV_HALO = 16


def _conv_kernel(prev_ref, cur_ref, next_ref, w_ref, b_ref, o_ref, ext_ref, *, tiles_per_seq, tc):
    i = pl.program_id(0)
    pos = i % tiles_per_seq
    zeros = jnp.zeros((CONV_HALO, CONV_DIM), F32)
    ext_ref[0:CONV_HALO, :] = jnp.where(pos == 0, zeros, prev_ref[...].astype(F32))
    ext_ref[CONV_HALO:CONV_HALO + tc, :] = cur_ref[...].astype(F32)
    ext_ref[CONV_HALO + tc:, :] = jnp.where(pos == tiles_per_seq - 1, zeros, next_ref[...].astype(F32))
    pad = (D_CONV - 1) // 2
    acc = jnp.zeros((tc, CONV_DIM), F32) + b_ref[...]
    for j in range(D_CONV):
        start = CONV_HALO - pad + j
        acc = acc + ext_ref[start:start + tc, :] * w_ref[j:j + 1, :]
    o_ref[...] = _silu(acc).astype(o_ref.dtype)


def _conv(p, conv_w, conv_b, seq_len, tc):
    T = p.shape[0]
    n_tiles = T // tc
    hb = tc // CONV_HALO
    n_hblocks = T // CONV_HALO
    colb = OFF_XBC // CONV_DIM
    return pl.pallas_call(
        functools.partial(_conv_kernel, tiles_per_seq=seq_len // tc, tc=tc),
        out_shape=jax.ShapeDtypeStruct((T, CONV_DIM), BF16),
        grid=(n_tiles,),
        in_specs=[pl.BlockSpec((CONV_HALO, CONV_DIM), lambda i: (jnp.maximum(i * hb - 1, 0), colb)),
                  pl.BlockSpec((tc, CONV_DIM), lambda i: (i, colb)),
                  pl.BlockSpec((CONV_HALO, CONV_DIM),
                               lambda i: (jnp.minimum((i + 1) * hb, n_hblocks - 1), colb)),
                  pl.BlockSpec((D_CONV, CONV_DIM), lambda i: (0, 0)),
                  pl.BlockSpec((1, CONV_DIM), lambda i: (0, 0))],
        out_specs=pl.BlockSpec((tc, CONV_DIM), lambda i: (i, 0)),
        scratch_shapes=[pltpu.VMEM((tc + 2 * CONV_HALO, CONV_DIM), F32)],
        compiler_params=_cparams(("arbitrary",)),
        name="dwconv_silu",
    )(p, p, p, conv_w, conv_b)


N_CTX_CHUNKS = 2


SSD_BLOCK = 512


def _ssd_chunk(x_cols, dt_raw, dtb, alog, dskip_ref, e_ref, ht_ref, y_ref, d):
    reverse = d == 1
    dt = jax.nn.softplus(dt_raw + dtb)
    a = dt * (-jnp.exp(alog))

    row = lax.broadcasted_iota(jnp.int32, (CHUNK, CHUNK), 0)
    col = lax.broadcasted_iota(jnp.int32, (CHUNK, CHUNK), 1)
    mask = (row <= col) if reverse else (row >= col)
    tri = mask.astype(BF16)
    tri_t = ((row >= col) if reverse else (row <= col)).astype(BF16)

    a_cs = _split_dot_rhs(tri, a)
    a_cs_t = _split_dot(a.T, tri_t)
    end = 0 if reverse else CHUNK - 1

    lane = lax.broadcasted_iota(jnp.int32, (CHUNK, 2 * SSM_HEAD_DIM), 1)
    left = lane < SSM_HEAD_DIM
    gw = D_MODEL // SSM_GROUPS
    heads_per_group = SSM_HEADS // SSM_GROUPS
    heads_per_block = SSD_BLOCK // SSM_HEAD_DIM
    for g in range(SSM_GROUPS):
        b_g = x_cols(D_MODEL + g * SSM_STATE, D_MODEL + (g + 1) * SSM_STATE)
        c_g = x_cols(D_MODEL + (SSM_GROUPS + g) * SSM_STATE, D_MODEL + (SSM_GROUPS + g + 1) * SSM_STATE)
        cb = lax.dot_general(c_g, b_g, (((1,), (1,)), ((), ())), preferred_element_type=F32)
        b_t = b_g.astype(F32).T.astype(BF16)
        for blk in range(gw // SSD_BLOCK):
            c0 = g * gw + blk * SSD_BLOCK
            cols = slice(c0, c0 + SSD_BLOCK)
            gcols = slice(blk * SSD_BLOCK, (blk + 1) * SSD_BLOCK)
            e_blk = e_ref[:, cols]
            a_cs_x = _split_dot(a_cs, e_blk, terms=2)
            dt_x = _split_dot(dt, e_blk, terms=1)
            a_tot_x = a_cs_x[end:end + 1, :]
            xs = x_cols(c0, c0 + SSD_BLOCK).astype(F32)
            xdt = xs * dt_x
            xdt_bf = xdt.astype(BF16)
            xw = (xdt * jnp.exp(a_tot_x - a_cs_x)).astype(BF16)
            h_in = ht_ref[g, :, gcols]
            y_blk = jnp.dot(c_g, h_in.astype(BF16), preferred_element_type=F32) * jnp.exp(a_cs_x)
            ht_ref[g, :, gcols] = h_in * jnp.exp(a_tot_x) + jnp.dot(b_t, xw, preferred_element_type=F32)
            y_pairs = []
            for j in range(heads_per_block // 2):
                ms = []
                for k in range(2):
                    hh = d * SSM_HEADS + g * heads_per_group + blk * heads_per_block + 2 * j + k
                    seg = a_cs[:, hh:hh + 1] - a_cs_t[hh:hh + 1, :]
                    dec = jnp.exp(jnp.where(mask, seg, NEG_BIG))
                    ms.append((cb * dec).astype(BF16))
                lhs = jnp.concatenate(ms, axis=1)
                xp = xdt_bf[:, 2 * j * SSM_HEAD_DIM:(2 * j + 2) * SSM_HEAD_DIM]
                zero = jnp.zeros_like(xp)
                rhs = jnp.concatenate([jnp.where(left, xp, zero), jnp.where(left, zero, xp)], axis=0)
                y_pairs.append(jnp.dot(lhs, rhs, preferred_element_type=F32))
            y_blk = y_blk + jnp.concatenate(y_pairs, axis=1) + dskip_ref[:, cols] * xs
            y_ref[:, cols] = y_blk.astype(y_ref.dtype)


def _ssd_kernel(xcf_ref, xlf_ref, dtcf_ref, dtlf_ref, xcr_ref, xlr_ref, dtcr_ref, dtlr_ref,
                dtb_ref, alog_ref, dskip_ref, e_ref,
                ycf_ref, ylf_ref, ycr_ref, ylr_ref, ht_ref, y_ref):
    s = pl.program_id(0)
    is_ctx = s < N_CTX_CHUNKS
    batch = xcf_ref.shape[0]

    @pl.when(s == 0)
    def _():
        ht_ref[...] = jnp.zeros_like(ht_ref)

    sides = ((xcf_ref, xlf_ref, dtcf_ref, dtlf_ref, ycf_ref, ylf_ref),
             (xcr_ref, xlr_ref, dtcr_ref, dtlr_ref, ycr_ref, ylr_ref))
    for b in range(batch):
        for d, (xc_ref, xl_ref, dtc_ref, dtl_ref, _, _) in enumerate(sides):
            x_cols = lambda c0, c1, b=b, xc_ref=xc_ref, xl_ref=xl_ref: jnp.where(
                is_ctx, xc_ref[b, :, c0:c1], xl_ref[b, :, c0:c1])
            dt_raw = jnp.where(is_ctx, dtc_ref[b, :, 0:CHUNK], dtl_ref[b, :, 0:CHUNK])
            _ssd_chunk(x_cols, dt_raw, dtb_ref[...], alog_ref[...], dskip_ref.at[d], e_ref.at[d],
                       ht_ref.at[b, d], y_ref.at[b, d], d)

    @pl.when(is_ctx)
    def _():
        for d in range(2):
            sides[d][4][...] = y_ref[:, d]

    @pl.when(jnp.logical_not(is_ctx))
    def _():
        for d in range(2):
            sides[d][5][...] = y_ref[:, d]


def _ssd(xconv_ctx, xconv_lat, dt_ctx, dt_lat, dtb, alog, dskip_x, e_mats, batch):
    n_lat = xconv_lat.shape[0] // batch // CHUNK
    n_steps = N_CTX_CHUNKS + n_lat
    per_batch = lambda t: t.reshape(batch, t.shape[0] // batch, t.shape[1])

    cidx = (lambda s: jnp.minimum(s, N_CTX_CHUNKS - 1),
            lambda s: jnp.maximum(N_CTX_CHUNKS - 1 - s, 0))
    lidx = (lambda s: jnp.maximum(s - N_CTX_CHUNKS, 0),
            lambda s: jnp.minimum(n_steps - 1 - s, n_lat - 1))

    def specs(width, d):
        return [pl.BlockSpec((batch, CHUNK, width), lambda s, d=d: (0, cidx[d](s), 0)),
                pl.BlockSpec((batch, CHUNK, width), lambda s, d=d: (0, lidx[d](s), 0))]

    const = lambda shape: pl.BlockSpec(shape, lambda s: (0,) * len(shape))
    in_specs, out_specs, args = [], [], []
    for d in range(2):
        in_specs += specs(CONV_DIM, d) + specs(DT_PAD, d)
        args += [per_batch(xconv_ctx), per_batch(xconv_lat), per_batch(dt_ctx), per_batch(dt_lat)]
        out_specs += specs(D_MODEL, d)
    in_specs += [const((1, CHUNK)), const((1, CHUNK)), const((2, 1, D_MODEL)), const((2, CHUNK, D_MODEL))]
    y_shapes = (jax.ShapeDtypeStruct((batch, xconv_ctx.shape[0] // batch, D_MODEL), BF16),
                jax.ShapeDtypeStruct((batch, xconv_lat.shape[0] // batch, D_MODEL), BF16))
    ys = pl.pallas_call(
        _ssd_kernel,
        out_shape=y_shapes + y_shapes,
        grid=(n_steps,),
        in_specs=in_specs,
        out_specs=tuple(out_specs),
        scratch_shapes=[pltpu.VMEM((batch, 2, SSM_GROUPS, SSM_STATE, D_MODEL // SSM_GROUPS), F32),
                        pltpu.VMEM((batch, 2, CHUNK, D_MODEL), BF16)],
        compiler_params=_cparams(("arbitrary",)),
        name="ssd_scan_bidir",
    )(*args, dtb, alog, dskip_x, e_mats)
    ycf, ylf, ycr, ylr = (y.reshape(-1, D_MODEL) for y in ys)
    return (ycf, ylf), (ycr, ylr)


def _norm_rope(t, g, cos, sin):
    tn = t * lax.rsqrt(jnp.mean(t * t, axis=-1, keepdims=True) + RMS_EPS) * g
    lane = lax.broadcasted_iota(jnp.int32, tn.shape, 1)
    partner = jnp.where(lane % 2 == 0, pltpu.roll(tn, HEAD_DIM - 1, axis=1), pltpu.roll(tn, 1, axis=1))
    return tn * cos + partner * sin


def _kvprep_kernel(k_ref, v_ref, cos_ref, sin_ref, g_ref, ko_ref, vto_ref):
    for h in range(N_KV_HEADS):
        cols = slice(h * HEAD_DIM, (h + 1) * HEAD_DIM)
        kh = _norm_rope(k_ref[:, cols].astype(F32), g_ref[...], cos_ref[...], sin_ref[...])
        ko_ref[h] = kh.astype(BF16)
        vto_ref[h, 0:HEAD_DIM, :] = v_ref[:, cols].astype(F32).T.astype(BF16)
        vto_ref[h, HEAD_DIM:, :] = jnp.ones((VT_ROWS - HEAD_DIM, k_ref.shape[0]), BF16)


def _kvprep(p, cos, sin, k_g, batch, tm):
    T = p.shape[0]
    S = T // batch
    nt = S // tm
    return pl.pallas_call(
        _kvprep_kernel,
        out_shape=(jax.ShapeDtypeStruct((batch, N_KV_HEADS, S, HEAD_DIM), BF16),
                   jax.ShapeDtypeStruct((batch, N_KV_HEADS, VT_ROWS, S), BF16)),
        grid=(batch, nt),
        in_specs=[pl.BlockSpec((tm, KV_WIDTH), lambda b, i: (b * nt + i, OFF_K // KV_WIDTH)),
                  pl.BlockSpec((tm, KV_WIDTH), lambda b, i: (b * nt + i, OFF_VV // KV_WIDTH)),
                  pl.BlockSpec((tm, HEAD_DIM), lambda b, i: (i, 0)),
                  pl.BlockSpec((tm, HEAD_DIM), lambda b, i: (i, 0)),
                  pl.BlockSpec((1, HEAD_DIM), lambda b, i: (0, 0))],
        out_specs=(pl.BlockSpec((None, N_KV_HEADS, tm, HEAD_DIM), lambda b, i: (b, 0, i, 0)),
                   pl.BlockSpec((None, N_KV_HEADS, VT_ROWS, tm), lambda b, i: (b, 0, 0, i))),
        compiler_params=_cparams(("arbitrary", "arbitrary")),
        name="kv_prep",
    )(p, p, cos, sin, k_g)


LOG2E = 1.4426950408889634


def _attn_kernel(*refs, tq, tk, n_lat_chunks):
    if n_lat_chunks:
        (q_ref, gc_ref, cos_ref, sin_ref, g_ref, kc_ref, vtc_ref, kl_ref, vtl_ref, o_ref,
         qt_ref, acc_ref, s0_ref, s1_ref) = refs
    else:
        q_ref, gc_ref, cos_ref, sin_ref, g_ref, kc_ref, vtc_ref, o_ref, qt_ref, acc_ref = refs
    scale = HEAD_DIM ** -0.5 * LOG2E
    for h in range(GQA_REP):
        cols = slice(h * HEAD_DIM, (h + 1) * HEAD_DIM)
        qh = _norm_rope(q_ref[:, cols].astype(F32), g_ref[...], cos_ref[...], sin_ref[...]) * scale
        qt_ref[:, h * tq:(h + 1) * tq] = qh.T.astype(BF16)

    def scores(k):
        return jnp.dot(k, qt_ref[...], preferred_element_type=F32)

    st = scores(kc_ref[...])
    m = jnp.max(st, axis=0, keepdims=True)
    p = jnp.exp2(st - m)
    acc_ref[...] = jnp.dot(vtc_ref[...], p.astype(BF16), preferred_element_type=F32)

    if n_lat_chunks:
        def offset(j):
            return j * tk if isinstance(j, int) else pl.multiple_of(j * tk, tk)

        def k_chunk(j):
            return kl_ref[pl.ds(offset(j), tk), :]

        def vt_chunk(j):
            return vtl_ref[:, pl.ds(offset(j), tk)]

        def scores_into(s_ref, k):
            st = scores(k)
            s_ref[...] = st
            return jnp.max(st, axis=0, keepdims=True)

        def update(s_ref, vt, m, cmax):
            m_new = jnp.maximum(m, cmax)
            alpha = jnp.exp2(m - m_new)
            p = jnp.exp2(s_ref[...] - m_new)
            acc_ref[...] = alpha * acc_ref[...] + jnp.dot(vt, p.astype(BF16), preferred_element_type=F32)
            return m_new

        cm0 = scores_into(s0_ref, k_chunk(0))

        def pair(jj, carry):
            m, cm0 = carry
            j = 2 * jj
            cm1 = scores_into(s1_ref, k_chunk(j + 1))
            m = update(s0_ref, vt_chunk(j), m, cm0)
            cm0 = scores_into(s0_ref, k_chunk(j + 2))
            m = update(s1_ref, vt_chunk(j + 1), m, cm1)
            return m, cm0

        m, cm0 = lax.fori_loop(0, n_lat_chunks // 2 - 1, pair, (m, cm0))
        j = n_lat_chunks - 2
        cm1 = scores_into(s1_ref, k_chunk(j + 1))
        m = update(s0_ref, vt_chunk(j), m, cm0)
        m = update(s1_ref, vt_chunk(j + 1), m, cm1)

    inv = 1.0 / acc_ref[HEAD_DIM:HEAD_DIM + 1, :]
    for h in range(GQA_REP):
        cols = slice(h * HEAD_DIM, (h + 1) * HEAD_DIM)
        lanes = slice(h * tq, (h + 1) * tq)
        oh = (acc_ref[0:HEAD_DIM, lanes] * inv[:, lanes]).T
        o_ref[:, cols] = (oh * _silu(gc_ref[:, cols].astype(F32))).astype(o_ref.dtype)


def _attention(p_q, cos, sin, q_g, k_ctx, vt_ctx, k_lat, vt_lat, batch, tq, tk):
    T = p_q.shape[0]
    S = T // batch
    nq = S // tq
    n_ctx = k_ctx.shape[2]
    n_lat_chunks = 0 if k_lat is None else k_lat.shape[2] // tk
    assert n_lat_chunks % 2 == 0
    gw = GQA_REP * HEAD_DIM
    in_specs = [pl.BlockSpec((tq, gw), lambda b, g, i: (b * nq + i, OFF_Q // gw + g)),
                pl.BlockSpec((tq, gw), lambda b, g, i: (b * nq + i, OFF_GC // gw + g)),
                pl.BlockSpec((tq, HEAD_DIM), lambda b, g, i: (i, 0)),
                pl.BlockSpec((tq, HEAD_DIM), lambda b, g, i: (i, 0)),
                pl.BlockSpec((1, HEAD_DIM), lambda b, g, i: (0, 0)),
                pl.BlockSpec((None, None, n_ctx, HEAD_DIM), lambda b, g, i: (b, g, 0, 0)),
                pl.BlockSpec((None, None, VT_ROWS, n_ctx), lambda b, g, i: (b, g, 0, 0))]
    args = [p_q, p_q, cos, sin, q_g, k_ctx, vt_ctx]
    scratch = [pltpu.VMEM((HEAD_DIM, GQA_REP * tq), BF16), pltpu.VMEM((VT_ROWS, GQA_REP * tq), F32)]
    if n_lat_chunks:
        n_lat = k_lat.shape[2]
        in_specs += [pl.BlockSpec((None, None, n_lat, HEAD_DIM), lambda b, g, i: (b, g, 0, 0)),
                     pl.BlockSpec((None, None, VT_ROWS, n_lat), lambda b, g, i: (b, g, 0, 0))]
        args += [k_lat, vt_lat]
        scratch += [pltpu.VMEM((tk, GQA_REP * tq), F32), pltpu.VMEM((tk, GQA_REP * tq), F32)]
    return pl.pallas_call(
        functools.partial(_attn_kernel, tq=tq, tk=tk, n_lat_chunks=n_lat_chunks),
        out_shape=jax.ShapeDtypeStruct((T, D_MODEL), BF16),
        grid=(batch, N_KV_HEADS, nq),
        in_specs=in_specs,
        out_specs=pl.BlockSpec((tq, gw), lambda b, g, i: (b * nq + i, g)),
        scratch_shapes=scratch,
        compiler_params=_cparams(("arbitrary", "arbitrary", "arbitrary")),
        name="gqa_flash_lat" if n_lat_chunks else "gqa_flash_ctx",
    )(*args)


def _merge_kernel(u_ref, v_ref, ga_ref, sw_ref, sbias_ref, sgain_ref, sf_ref, sb_ref, z_ref, yc_ref, gm_ref,
                  x_ref, mod_ref, wb_ref, wo_ref, bo_ref, sg_ref, lg_ref, lb_ref, out_ref, ya_ref):
    _sgu_kernel(u_ref, v_ref, ga_ref, sw_ref, sbias_ref, sgain_ref, ya_ref, n_chunks=u_ref.shape[0] // CHUNK)
    s = (sf_ref[...].astype(F32) + sb_ref[...].astype(F32)) * _silu(z_ref[...].astype(F32))
    yb = s * lax.rsqrt(jnp.mean(s * s, axis=-1, keepdims=True) + RMS_EPS) * sg_ref[...]
    gate_of = lambda k: jax.nn.sigmoid(gm_ref[:, k * D_MODEL:(k + 1) * D_MODEL].astype(F32))
    merged = gate_of(0) * jnp.dot(ya_ref[...], wb_ref[0], preferred_element_type=F32)
    merged = merged + gate_of(1) * jnp.dot(yb.astype(BF16), wb_ref[1], preferred_element_type=F32)
    merged = merged + gate_of(2) * jnp.dot(yc_ref[...], wb_ref[2], preferred_element_type=F32)
    out = jnp.dot(merged.astype(BF16), wo_ref[...], preferred_element_type=F32) + bo_ref[...]
    gate = mod_ref[:, 2 * D_MODEL:3 * D_MODEL]
    r = DEEPNORM_ALPHA * x_ref[...] + gate * out
    mu = jnp.mean(r, axis=-1, keepdims=True)
    rc = r - mu
    var = jnp.mean(rc * rc, axis=-1, keepdims=True)
    out_ref[...] = rc * lax.rsqrt(var + LN_EPS) * lg_ref[...] + lb_ref[...]


def _merge(sgu_w, sgu_bfull, sgu_g, s_f, s_b, p, y_c, x2d, mod, w_branch, w_out, b_out, ssm_g, ln_g, ln_b,
           tm, mod_row):
    T = x2d.shape[0]
    row = lambda: pl.BlockSpec((tm, D_MODEL), lambda i: (i, 0))
    pcol = lambda off: pl.BlockSpec((tm, D_MODEL), lambda i: (i, off // D_MODEL))
    vec = lambda: pl.BlockSpec((1, D_MODEL), lambda i: (0, 0))
    once = pl.Buffered(1)
    return pl.pallas_call(
        _merge_kernel,
        out_shape=jax.ShapeDtypeStruct((T, D_MODEL), F32),
        grid=(T // tm,),
        in_specs=[pcol(OFF_U), pcol(OFF_V), pcol(OFF_GA),
                  pl.BlockSpec((A_GROUPS, CHUNK, CHUNK), lambda i: (0, 0, 0)),
                  pl.BlockSpec((CHUNK, D_MODEL), lambda i: (0, 0)),
                  vec(),
                  row(), row(), pcol(OFF_Z), row(),
                  pl.BlockSpec((tm, 3 * D_MODEL), lambda i: (i, OFF_GM // (3 * D_MODEL))),
                  row(),
                  pl.BlockSpec((None, 1, 3 * D_MODEL), lambda i: (mod_row(i), 0, 0)),
                  pl.BlockSpec((3, D_MODEL, D_MODEL), lambda i: (0, 0, 0), pipeline_mode=once),
                  pl.BlockSpec((D_MODEL, D_MODEL), lambda i: (0, 0), pipeline_mode=once),
                  vec(), vec(), vec(), vec()],
        out_specs=row(),
        scratch_shapes=[pltpu.VMEM((tm, D_MODEL), BF16)],
        compiler_params=_cparams(("arbitrary",)),
        name="sgu_merge_out_deepnorm",
    )(p, p, p, sgu_w, sgu_bfull, sgu_g, s_f, s_b, p, y_c, p, x2d, mod, w_branch, w_out, b_out, ssm_g, ln_g, ln_b)


def _pack_kernel(w_ref, o_ref):
    x = w_ref[...]
    o = 0
    seg = {}
    for name, n in (("u", 1024), ("v", 1024), ("ga", 1024), ("z", 1024), ("xbc", CONV_DIM),
                    ("dt", 2 * SSM_HEADS), ("q", 1024), ("k", KV_WIDTH), ("vv", KV_WIDTH),
                    ("gc", 1024), ("gm", 3 * D_MODEL)):
        seg[name] = x[:, o:o + n]
        o += n
    pad = jnp.zeros((x.shape[0], DT_PAD - 2 * SSM_HEADS), x.dtype)
    packed = jnp.concatenate(
        [seg["u"], seg["v"], seg["ga"], seg["z"], seg["q"], seg["gc"], seg["gm"], seg["xbc"],
         seg["k"], seg["vv"], seg["dt"], pad], axis=1)
    o_ref[...] = packed.astype(BF16)


def _pack_w_in(w_in, tr=128):
    L, rows, n_in = w_in.shape
    return pl.pallas_call(
        _pack_kernel,
        out_shape=jax.ShapeDtypeStruct((L, rows, N_PACK), BF16),
        grid=(L, rows // tr),
        in_specs=[pl.BlockSpec((None, tr, n_in), lambda l, i: (l, i, 0))],
        out_specs=pl.BlockSpec((None, tr, N_PACK), lambda l, i: (l, i, 0)),
        compiler_params=_cparams(("arbitrary", "arbitrary")),
        name="pack_w_in",
    )(w_in)


def _rope_tables(n_tokens):
    rows = n_tokens // GRID_W
    row = np.repeat(np.arange(rows, dtype=np.float32), GRID_W)
    col = np.tile(np.arange(GRID_W, dtype=np.float32), rows)
    inv = np.float32(ROPE_THETA) ** (-np.arange(ROPE_PAIRS_PER_AXIS, dtype=np.float32) / ROPE_PAIRS_PER_AXIS)
    ang = np.concatenate([row[:, None] * inv, col[:, None] * inv], axis=-1).astype(np.float32)
    cos, sin = np.cos(ang), np.sin(ang)
    cos_f = np.stack([cos, cos], axis=-1).reshape(n_tokens, HEAD_DIM)
    sin_f = np.stack([-sin, sin], axis=-1).reshape(n_tokens, HEAD_DIM)
    return jnp.asarray(cos_f, F32), jnp.asarray(sin_f, F32)


def _pad_lanes(v, n):
    return jnp.concatenate([v, jnp.zeros((n - v.shape[0],), v.dtype)])[None, :]


def kernel(x, c, ctx, c_ctx, w_mod, b_mod, w_in, sgu_w, sgu_b, sgu_norm_g, conv_w, conv_b, dt_bias,
           a_log, d_skip, ssm_norm_g, q_norm_g, k_norm_g, w_branch, w_out, b_out, ln_g, ln_b):
    batch, seq, _ = x.shape
    n_ctx = ctx.shape[1]
    assert n_ctx == N_CTX_CHUNKS * CHUNK and seq % 1024 == 0

    cos_l, sin_l = _rope_tables(seq)
    cos_c = jnp.ones((n_ctx, HEAD_DIM), F32)
    sin_c = jnp.zeros((n_ctx, HEAD_DIM), F32)

    c_rows = jnp.concatenate([c, c_ctx[None, :], jnp.zeros((8 - batch - 1, D_MODEL), F32)], axis=0)
    mod_all = _modulation(c_rows, w_mod, b_mod[:, None, :])
    ctx_row = batch

    hid = jnp.arange(CHUNK)[:, None]
    ch = jnp.arange(D_MODEL)[None, :] // SSM_HEAD_DIM
    e_mats = jnp.stack([(hid == ch + d * SSM_HEADS).astype(BF16) for d in range(2)])

    lat_per_tile = 1024
    tiles_per_batch = seq // lat_per_tile
    xl = x.reshape(batch * seq, D_MODEL)
    xc = ctx.reshape(batch * n_ctx, D_MODEL)

    w_pack_all = _pack_w_in(w_in)

    for l in range(DEPTH):
        ctx_out = l < DEPTH - 1
        mod = mod_all[l][:, None, :]
        wb = w_branch[l].astype(BF16)
        wo = w_out[l].astype(BF16)
        q_g, k_g = q_norm_g[l][None, :], k_norm_g[l][None, :]
        sgu_wb = sgu_w[l].astype(BF16)
        sgu_bfull = jnp.repeat(sgu_b[l].T, A_GROUP_DIM, axis=1)
        sgu_g = sgu_norm_g[l][None, :]
        dtb = _pad_lanes(dt_bias[l].reshape(-1), CHUNK)
        alog = _pad_lanes(a_log[l].reshape(-1), CHUNK)

        lat_row = lambda i: i // tiles_per_batch
        p_lat, dt_lat = _inproj(xl, mod, w_pack_all, l, lat_per_tile, lat_row)
        p_ctx, dt_ctx = _inproj(xc, mod, w_pack_all, l, n_ctx, lambda i: ctx_row)

        xconv_lat = _conv(p_lat, conv_w[l], conv_b[l][None, :], seq, 512)
        xconv_ctx = _conv(p_ctx, conv_w[l], conv_b[l][None, :], n_ctx, n_ctx)
        dskip_x = jnp.repeat(d_skip[l], SSM_HEAD_DIM, axis=1)[:, None, :]
        ssd = _ssd(xconv_ctx, xconv_lat, dt_ctx, dt_lat, dtb, alog, dskip_x, e_mats, batch)

        k_ctx, vt_ctx = _kvprep(p_ctx, cos_c, sin_c, k_g, batch, n_ctx)
        k_lat, vt_lat = _kvprep(p_lat, cos_l, sin_l, k_g, batch, 512)
        o_lat = _attention(p_lat, cos_l, sin_l, q_g, k_ctx, vt_ctx, k_lat, vt_lat, batch, 512, 1024)

        sgu = (sgu_wb, sgu_bfull, sgu_g)
        vecs = (b_out[l][None, :], ssm_norm_g[l][None, :], ln_g[l][None, :], ln_b[l][None, :])
        merge_tm = 512
        merge_tiles_per_batch = seq // merge_tm
        xl_new = _merge(*sgu, ssd[0][1], ssd[1][1], p_lat, o_lat, xl, mod, wb, wo, *vecs,
                        merge_tm, lambda i: i // merge_tiles_per_batch)
        if ctx_out:
            o_ctx = _attention(p_ctx, cos_c, sin_c, q_g, k_ctx, vt_ctx, None, None, batch, n_ctx, 0)
            xc = _merge(*sgu, ssd[0][0], ssd[1][0], p_ctx, o_ctx, xc, mod, wb, wo, *vecs,
                        n_ctx, lambda i: ctx_row)
        xl = xl_new

    return xl.reshape(batch, seq, D_MODEL)
```

```python
import functools
import math

import jax
import jax.numpy as jnp
import numpy as np
from jax import lax
from jax.experimental import pallas as pl
from jax.experimental.pallas import tpu as pltpu

D_MODEL = 1024
DEPTH = 2
GRID_W = 64
CHUNK = 128
A_GROUPS = 4
A_GROUP_DIM = D_MODEL // A_GROUPS
SSM_HEAD_DIM = 64
SSM_HEADS = 16
SSM_GROUPS = 2
SSM_STATE = 128
D_CONV = 5
CONV_DIM = D_MODEL + 2 * SSM_GROUPS * SSM_STATE
HEAD_DIM = 128
N_Q_HEADS = 8
N_KV_HEADS = 2
GQA_REP = N_Q_HEADS // N_KV_HEADS
KV_WIDTH = N_KV_HEADS * HEAD_DIM
ROPE_THETA = 10000.0
ROPE_PAIRS_PER_AXIS = HEAD_DIM // 4
DEEPNORM_ALPHA = (2 * DEPTH) ** 0.25
LN_EPS = 1e-6
RMS_EPS = 1e-6

OFF_U, OFF_V, OFF_GA, OFF_Z, OFF_Q, OFF_GC, OFF_GM = 0, 1024, 2048, 3072, 4096, 5120, 6144
OFF_XBC = 9216
OFF_K = OFF_XBC + CONV_DIM
OFF_VV = OFF_K + KV_WIDTH
OFF_DT = OFF_VV + KV_WIDTH
DT_PAD = 256
N_PACK = OFF_DT + DT_PAD
PROJ_TN = 1280

BF16_SUBLANES = 16
VT_ROWS = HEAD_DIM + BF16_SUBLANES

VMEM_LIMIT = 56 * 1024 * 1024
F32 = jnp.float32
BF16 = jnp.bfloat16
NEG_BIG = -1e30


def _cparams(sem):
    return pltpu.CompilerParams(dimension_semantics=sem, vmem_limit_bytes=VMEM_LIMIT)


def _silu(t):
    return t * jax.nn.sigmoid(t)


def _split_dot(x, e, terms=3):
    acc = None
    r = x
    for t in range(terms):
        xt = r.astype(BF16)
        if t + 1 < terms:
            r = r - xt.astype(F32)
        part = jnp.dot(xt, e, preferred_element_type=F32)
        acc = part if acc is None else acc + part
    return acc


def _split_dot_rhs(e, x):
    x1 = x.astype(BF16)
    r1 = x - x1.astype(F32)
    x2 = r1.astype(BF16)
    r2 = r1 - x2.astype(F32)
    x3 = r2.astype(BF16)
    acc = jnp.dot(e, x1, preferred_element_type=F32)
    acc = acc + jnp.dot(e, x2, preferred_element_type=F32)
    return acc + jnp.dot(e, x3, preferred_element_type=F32)


def _mod_kernel(c_ref, w_ref, b_ref, o_ref):
    c = c_ref[...]
    act = _silu(c)
    o_ref[...] = jnp.dot(act, w_ref[...], preferred_element_type=F32,
                         precision=lax.Precision.HIGHEST) + b_ref[...]


def _modulation(c_rows, w_mod, b_mod):
    L = w_mod.shape[0]
    tn = 512
    return pl.pallas_call(
        _mod_kernel,
        out_shape=jax.ShapeDtypeStruct((L, 8, 3 * D_MODEL), F32),
        grid=(L, 3 * D_MODEL // tn),
        in_specs=[pl.BlockSpec((8, D_MODEL), lambda l, j: (0, 0)),
                  pl.BlockSpec((None, D_MODEL, tn), lambda l, j: (l, 0, j)),
                  pl.BlockSpec((None, 1, tn), lambda l, j: (l, 0, j))],
        out_specs=pl.BlockSpec((None, 8, tn), lambda l, j: (l, 0, j)),
        compiler_params=_cparams(("arbitrary", "arbitrary")),
        name="adaln_modulation",
    )(c_rows, w_mod, b_mod)


def _inproj_kernel(x_ref, mod_ref, w_ref, o_ref, dt_ref, h_ref):
    j = pl.program_id(1)

    @pl.when(j == 0)
    def _():
        x = x_ref[...]
        mu = jnp.mean(x, axis=-1, keepdims=True)
        xc = x - mu
        var = jnp.mean(xc * xc, axis=-1, keepdims=True)
        shift = mod_ref[:, 0:D_MODEL]
        scale = mod_ref[:, D_MODEL:2 * D_MODEL]
        h = xc * lax.rsqrt(var + LN_EPS) * (1.0 + scale) + shift
        h_ref[...] = h.astype(BF16)

    acc = lax.dot_general(h_ref[...], w_ref[...], (((1,), (1,)), ((), ())), preferred_element_type=F32)
    o_ref[...] = acc.astype(o_ref.dtype)

    @pl.when(j == pl.num_programs(1) - 1)
    def _():
        dt_ref[...] = acc[:, PROJ_TN - DT_PAD:]


def _inproj(x2d, mod, w_pack, layer, tm, mod_row):
    T = x2d.shape[0]
    assert OFF_DT + DT_PAD == N_PACK and N_PACK % PROJ_TN == 0
    return pl.pallas_call(
        _inproj_kernel,
        out_shape=(jax.ShapeDtypeStruct((T, N_PACK), BF16),
                   jax.ShapeDtypeStruct((T, DT_PAD), F32)),
        grid=(T // tm, N_PACK // PROJ_TN),
        in_specs=[pl.BlockSpec((tm, D_MODEL), lambda i, j: (i, 0)),
                  pl.BlockSpec((None, 1, 3 * D_MODEL), lambda i, j: (mod_row(i), 0, 0)),
                  pl.BlockSpec((None, PROJ_TN, D_MODEL), lambda i, j: (layer, j, 0))],
        out_specs=(pl.BlockSpec((tm, PROJ_TN), lambda i, j: (i, j)),
                   pl.BlockSpec((tm, DT_PAD), lambda i, j: (i, 0))),
        scratch_shapes=[pltpu.VMEM((tm, D_MODEL), BF16)],
        compiler_params=_cparams(("arbitrary", "arbitrary")),
        name="ln_mod_inproj",
    )(x2d, mod, w_pack)


def _sgu_kernel(u_ref, v_ref, ga_ref, w_ref, b_ref, g_ref, o_ref, *, n_chunks):
    for c in range(n_chunks):
        rows = slice(c * CHUNK, (c + 1) * CHUNK)
        for g in range(A_GROUPS):
            cols = slice(g * A_GROUP_DIM, (g + 1) * A_GROUP_DIM)
            v = v_ref[rows, cols].astype(F32)
            mu = jnp.mean(v, axis=-1, keepdims=True)
            vc = v - mu
            var = jnp.mean(vc * vc, axis=-1, keepdims=True)
            vn = vc * lax.rsqrt(var + LN_EPS) * g_ref[:, cols]
            mixed = jnp.dot(w_ref[g], vn.astype(BF16), preferred_element_type=F32)
            mixed = mixed + b_ref[:, cols]
            y = u_ref[rows, cols].astype(F32) * mixed * _silu(ga_ref[rows, cols].astype(F32))
            o_ref[rows, cols] = y.astype(o_ref.dtype)


CONV_HALO = 64
CONV_WINDOW = CHUNK + 2 * CONV_HALO


def _conv_shift_matrix():
    pad = (D_CONV - 1) // 2
    m = np.zeros((D_CONV * CHUNK, CONV_WINDOW), np.float32)
    for j in range(D_CONV):
        t = np.arange(CHUNK)
        m[j * CHUNK + t, t + CONV_HALO + j - pad] = 1.0
    return jnp.asarray(m, BF16)


def _conv_kernel(prev_ref, cur_ref, next_ref, shift_ref, w_ref, b_ref, o_ref, ext_ref, *, tiles_per_seq, tc):
    i = pl.program_id(0)
    pos = i % tiles_per_seq
    zeros = jnp.zeros((CONV_HALO, CONV_DIM), BF16)
    ext_ref[0:CONV_HALO, :] = jnp.where(pos == 0, zeros, prev_ref[...])
    ext_ref[CONV_HALO:CONV_HALO + tc, :] = cur_ref[...]
    ext_ref[CONV_HALO + tc:, :] = jnp.where(pos == tiles_per_seq - 1, zeros, next_ref[...])
    for k in range(tc // CHUNK):
        window = ext_ref[k * CHUNK:k * CHUNK + CONV_WINDOW, :]
        taps = jnp.dot(shift_ref[...], window, preferred_element_type=F32)
        acc = b_ref[...] + taps[0:CHUNK, :] * w_ref[0:1, :]
        for j in range(1, D_CONV):
            acc = acc + taps[j * CHUNK:(j + 1) * CHUNK, :] * w_ref[j:j + 1, :]
        o_ref[k * CHUNK:(k + 1) * CHUNK, :] = _silu(acc).astype(o_ref.dtype)


def _conv(p, shift_mat, conv_w, conv_b, seq_len, tc):
    T = p.shape[0]
    n_tiles = T // tc
    hb = tc // CONV_HALO
    n_hblocks = T // CONV_HALO
    colb = OFF_XBC // CONV_DIM
    return pl.pallas_call(
        functools.partial(_conv_kernel, tiles_per_seq=seq_len // tc, tc=tc),
        out_shape=jax.ShapeDtypeStruct((T, CONV_DIM), BF16),
        grid=(n_tiles,),
        in_specs=[pl.BlockSpec((CONV_HALO, CONV_DIM), lambda i: (jnp.maximum(i * hb - 1, 0), colb)),
                  pl.BlockSpec((tc, CONV_DIM), lambda i: (i, colb)),
                  pl.BlockSpec((CONV_HALO, CONV_DIM),
                               lambda i: (jnp.minimum((i + 1) * hb, n_hblocks - 1), colb)),
                  pl.BlockSpec((D_CONV * CHUNK, CONV_WINDOW), lambda i: (0, 0)),
                  pl.BlockSpec((D_CONV, CONV_DIM), lambda i: (0, 0)),
                  pl.BlockSpec((1, CONV_DIM), lambda i: (0, 0))],
        out_specs=pl.BlockSpec((tc, CONV_DIM), lambda i: (i, 0)),
        scratch_shapes=[pltpu.VMEM((tc + 2 * CONV_HALO, CONV_DIM), BF16)],
        compiler_params=_cparams(("arbitrary",)),
        name="dwconv_silu",
    )(p, p, p, shift_mat, conv_w, conv_b)


N_CTX_CHUNKS = 2


SSD_BLOCK = 512


def _ssd_chunk(x_cols, dt_raw, dtb, alog, dskip_ref, e_ref, ht_ref, y_ref, d):
    reverse = d == 1
    dt = jax.nn.softplus(dt_raw + dtb)
    a = dt * (-jnp.exp(alog))

    row = lax.broadcasted_iota(jnp.int32, (CHUNK, CHUNK), 0)
    col = lax.broadcasted_iota(jnp.int32, (CHUNK, CHUNK), 1)
    mask = (row <= col) if reverse else (row >= col)
    tri = mask.astype(BF16)
    tri_t = ((row >= col) if reverse else (row <= col)).astype(BF16)

    a_cs = _split_dot_rhs(tri, a)
    a_cs_t = _split_dot(a.T, tri_t)
    end = 0 if reverse else CHUNK - 1

    lane = lax.broadcasted_iota(jnp.int32, (CHUNK, 2 * SSM_HEAD_DIM), 1)
    left = lane < SSM_HEAD_DIM
    gw = D_MODEL // SSM_GROUPS
    heads_per_group = SSM_HEADS // SSM_GROUPS
    heads_per_block = SSD_BLOCK // SSM_HEAD_DIM
    for g in range(SSM_GROUPS):
        b_g = x_cols(D_MODEL + g * SSM_STATE, D_MODEL + (g + 1) * SSM_STATE)
        c_g = x_cols(D_MODEL + (SSM_GROUPS + g) * SSM_STATE, D_MODEL + (SSM_GROUPS + g + 1) * SSM_STATE)
        cb = lax.dot_general(c_g, b_g, (((1,), (1,)), ((), ())), preferred_element_type=F32)
        b_t = b_g.astype(F32).T.astype(BF16)
        for blk in range(gw // SSD_BLOCK):
            c0 = g * gw + blk * SSD_BLOCK
            cols = slice(c0, c0 + SSD_BLOCK)
            gcols = slice(blk * SSD_BLOCK, (blk + 1) * SSD_BLOCK)
            e_blk = e_ref[:, cols]
            a_cs_x = _split_dot(a_cs, e_blk, terms=2)
            dt_x = _split_dot(dt, e_blk, terms=1)
            a_tot_x = a_cs_x[end:end + 1, :]
            xs = x_cols(c0, c0 + SSD_BLOCK).astype(F32)
            xdt = xs * dt_x
            xdt_bf = xdt.astype(BF16)
            xw = (xdt * jnp.exp(a_tot_x - a_cs_x)).astype(BF16)
            h_in = ht_ref[g, :, gcols]
            y_blk = jnp.dot(c_g, h_in.astype(BF16), preferred_element_type=F32) * jnp.exp(a_cs_x)
            ht_ref[g, :, gcols] = h_in * jnp.exp(a_tot_x) + jnp.dot(b_t, xw, preferred_element_type=F32)
            y_pairs = []
            for j in range(heads_per_block // 2):
                ms = []
                for k in range(2):
                    hh = d * SSM_HEADS + g * heads_per_group + blk * heads_per_block + 2 * j + k
                    seg = a_cs[:, hh:hh + 1] - a_cs_t[hh:hh + 1, :]
                    dec = jnp.exp(jnp.where(mask, seg, NEG_BIG))
                    ms.append((cb * dec).astype(BF16))
                lhs = jnp.concatenate(ms, axis=1)
                xp = xdt_bf[:, 2 * j * SSM_HEAD_DIM:(2 * j + 2) * SSM_HEAD_DIM]
                zero = jnp.zeros_like(xp)
                rhs = jnp.concatenate([jnp.where(left, xp, zero), jnp.where(left, zero, xp)], axis=0)
                y_pairs.append(jnp.dot(lhs, rhs, preferred_element_type=F32))
            y_blk = y_blk + jnp.concatenate(y_pairs, axis=1) + dskip_ref[:, cols] * xs
            y_ref[:, cols] = y_blk.astype(y_ref.dtype)


def _ssd_kernel(xcf_ref, xlf_ref, dtcf_ref, dtlf_ref, xcr_ref, xlr_ref, dtcr_ref, dtlr_ref,
                dtb_ref, alog_ref, dskip_ref, e_ref,
                ycf_ref, ylf_ref, ycr_ref, ylr_ref, ht_ref, y_ref):
    s = pl.program_id(0)
    is_ctx = s < N_CTX_CHUNKS
    batch = xcf_ref.shape[0]

    @pl.when(s == 0)
    def _():
        ht_ref[...] = jnp.zeros_like(ht_ref)

    sides = ((xcf_ref, xlf_ref, dtcf_ref, dtlf_ref, ycf_ref, ylf_ref),
             (xcr_ref, xlr_ref, dtcr_ref, dtlr_ref, ycr_ref, ylr_ref))
    for b in range(batch):
        for d, (xc_ref, xl_ref, dtc_ref, dtl_ref, _, _) in enumerate(sides):
            x_cols = lambda c0, c1, b=b, xc_ref=xc_ref, xl_ref=xl_ref: jnp.where(
                is_ctx, xc_ref[b, :, c0:c1], xl_ref[b, :, c0:c1])
            dt_raw = jnp.where(is_ctx, dtc_ref[b, :, 0:CHUNK], dtl_ref[b, :, 0:CHUNK])
            _ssd_chunk(x_cols, dt_raw, dtb_ref[...], alog_ref[...], dskip_ref.at[d], e_ref.at[d],
                       ht_ref.at[b, d], y_ref.at[b, d], d)

    @pl.when(is_ctx)
    def _():
        for d in range(2):
            sides[d][4][...] = y_ref[:, d]

    @pl.when(jnp.logical_not(is_ctx))
    def _():
        for d in range(2):
            sides[d][5][...] = y_ref[:, d]


def _ssd(xconv_ctx, xconv_lat, dt_ctx, dt_lat, dtb, alog, dskip_x, e_mats, batch):
    n_lat = xconv_lat.shape[0] // batch // CHUNK
    n_steps = N_CTX_CHUNKS + n_lat
    per_batch = lambda t: t.reshape(batch, t.shape[0] // batch, t.shape[1])

    cidx = (lambda s: jnp.minimum(s, N_CTX_CHUNKS - 1),
            lambda s: jnp.maximum(N_CTX_CHUNKS - 1 - s, 0))
    lidx = (lambda s: jnp.maximum(s - N_CTX_CHUNKS, 0),
            lambda s: jnp.minimum(n_steps - 1 - s, n_lat - 1))

    def specs(width, d):
        return [pl.BlockSpec((batch, CHUNK, width), lambda s, d=d: (0, cidx[d](s), 0)),
                pl.BlockSpec((batch, CHUNK, width), lambda s, d=d: (0, lidx[d](s), 0))]

    const = lambda shape: pl.BlockSpec(shape, lambda s: (0,) * len(shape))
    in_specs, out_specs, args = [], [], []
    for d in range(2):
        in_specs += specs(CONV_DIM, d) + specs(DT_PAD, d)
        args += [per_batch(xconv_ctx), per_batch(xconv_lat), per_batch(dt_ctx), per_batch(dt_lat)]
        out_specs += specs(D_MODEL, d)
    in_specs += [const((1, CHUNK)), const((1, CHUNK)), const((2, 1, D_MODEL)), const((2, CHUNK, D_MODEL))]
    y_shapes = (jax.ShapeDtypeStruct((batch, xconv_ctx.shape[0] // batch, D_MODEL), BF16),
                jax.ShapeDtypeStruct((batch, xconv_lat.shape[0] // batch, D_MODEL), BF16))
    ys = pl.pallas_call(
        _ssd_kernel,
        out_shape=y_shapes + y_shapes,
        grid=(n_steps,),
        in_specs=in_specs,
        out_specs=tuple(out_specs),
        scratch_shapes=[pltpu.VMEM((batch, 2, SSM_GROUPS, SSM_STATE, D_MODEL // SSM_GROUPS), F32),
                        pltpu.VMEM((batch, 2, CHUNK, D_MODEL), BF16)],
        compiler_params=_cparams(("arbitrary",)),
        name="ssd_scan_bidir",
    )(*args, dtb, alog, dskip_x, e_mats)
    ycf, ylf, ycr, ylr = (y.reshape(-1, D_MODEL) for y in ys)
    return (ycf, ylf), (ycr, ylr)


def _norm_rope(t, g, cos, sin):
    tn = t * lax.rsqrt(jnp.mean(t * t, axis=-1, keepdims=True) + RMS_EPS) * g
    lane = lax.broadcasted_iota(jnp.int32, tn.shape, 1)
    partner = jnp.where(lane % 2 == 0, pltpu.roll(tn, HEAD_DIM - 1, axis=1), pltpu.roll(tn, 1, axis=1))
    return tn * cos + partner * sin


def _kvprep_kernel(k_ref, v_ref, cos_ref, sin_ref, g_ref, ko_ref, vto_ref):
    for h in range(N_KV_HEADS):
        cols = slice(h * HEAD_DIM, (h + 1) * HEAD_DIM)
        kh = _norm_rope(k_ref[:, cols].astype(F32), g_ref[...], cos_ref[...], sin_ref[...])
        ko_ref[h] = kh.astype(BF16)
        vto_ref[h, 0:HEAD_DIM, :] = v_ref[:, cols].astype(F32).T.astype(BF16)
        vto_ref[h, HEAD_DIM:, :] = jnp.ones((VT_ROWS - HEAD_DIM, k_ref.shape[0]), BF16)


def _kvprep(p, cos, sin, k_g, batch, tm):
    T = p.shape[0]
    S = T // batch
    nt = S // tm
    return pl.pallas_call(
        _kvprep_kernel,
        out_shape=(jax.ShapeDtypeStruct((batch, N_KV_HEADS, S, HEAD_DIM), BF16),
                   jax.ShapeDtypeStruct((batch, N_KV_HEADS, VT_ROWS, S), BF16)),
        grid=(batch, nt),
        in_specs=[pl.BlockSpec((tm, KV_WIDTH), lambda b, i: (b * nt + i, OFF_K // KV_WIDTH)),
                  pl.BlockSpec((tm, KV_WIDTH), lambda b, i: (b * nt + i, OFF_VV // KV_WIDTH)),
                  pl.BlockSpec((tm, HEAD_DIM), lambda b, i: (i, 0)),
                  pl.BlockSpec((tm, HEAD_DIM), lambda b, i: (i, 0)),
                  pl.BlockSpec((1, HEAD_DIM), lambda b, i: (0, 0))],
        out_specs=(pl.BlockSpec((None, N_KV_HEADS, tm, HEAD_DIM), lambda b, i: (b, 0, i, 0)),
                   pl.BlockSpec((None, N_KV_HEADS, VT_ROWS, tm), lambda b, i: (b, 0, 0, i))),
        compiler_params=_cparams(("arbitrary", "arbitrary")),
        name="kv_prep",
    )(p, p, cos, sin, k_g)


LOG2E = 1.4426950408889634


def _attn_kernel(*refs, tq, tk, n_lat_chunks):
    if n_lat_chunks:
        (q_ref, gc_ref, cos_ref, sin_ref, g_ref, kc_ref, vtc_ref, kl_ref, vtl_ref, o_ref,
         qt_ref, acc_ref, s0_ref, s1_ref) = refs
    else:
        q_ref, gc_ref, cos_ref, sin_ref, g_ref, kc_ref, vtc_ref, o_ref, qt_ref, acc_ref = refs
    scale = HEAD_DIM ** -0.5 * LOG2E
    for h in range(GQA_REP):
        cols = slice(h * HEAD_DIM, (h + 1) * HEAD_DIM)
        qh = _norm_rope(q_ref[:, cols].astype(F32), g_ref[...], cos_ref[...], sin_ref[...]) * scale
        qt_ref[:, h * tq:(h + 1) * tq] = qh.T.astype(BF16)

    def scores(k):
        return jnp.dot(k, qt_ref[...], preferred_element_type=F32)

    st = scores(kc_ref[...])
    m = jnp.max(st, axis=0, keepdims=True)
    p = jnp.exp2(st - m)
    acc_ref[...] = jnp.dot(vtc_ref[...], p.astype(BF16), preferred_element_type=F32)

    if n_lat_chunks:
        def offset(j):
            return j * tk if isinstance(j, int) else pl.multiple_of(j * tk, tk)

        def k_chunk(j):
            return kl_ref[pl.ds(offset(j), tk), :]

        def vt_chunk(j):
            return vtl_ref[:, pl.ds(offset(j), tk)]

        def scores_into(s_ref, k):
            st = scores(k)
            s_ref[...] = st
            return jnp.max(st, axis=0, keepdims=True)

        def update(s_ref, vt, m, cmax):
            m_new = jnp.maximum(m, cmax)
            alpha = jnp.exp2(m - m_new)
            p = jnp.exp2(s_ref[...] - m_new)
            acc_ref[...] = alpha * acc_ref[...] + jnp.dot(vt, p.astype(BF16), preferred_element_type=F32)
            return m_new

        cm0 = scores_into(s0_ref, k_chunk(0))

        def pair(jj, carry):
            m, cm0 = carry
            j = 2 * jj
            cm1 = scores_into(s1_ref, k_chunk(j + 1))
            m = update(s0_ref, vt_chunk(j), m, cm0)
            cm0 = scores_into(s0_ref, k_chunk(j + 2))
            m = update(s1_ref, vt_chunk(j + 1), m, cm1)
            return m, cm0

        m, cm0 = lax.fori_loop(0, n_lat_chunks // 2 - 1, pair, (m, cm0))
        j = n_lat_chunks - 2
        cm1 = scores_into(s1_ref, k_chunk(j + 1))
        m = update(s0_ref, vt_chunk(j), m, cm0)
        m = update(s1_ref, vt_chunk(j + 1), m, cm1)

    inv = 1.0 / acc_ref[HEAD_DIM:HEAD_DIM + 1, :]
    for h in range(GQA_REP):
        cols = slice(h * HEAD_DIM, (h + 1) * HEAD_DIM)
        lanes = slice(h * tq, (h + 1) * tq)
        oh = (acc_ref[0:HEAD_DIM, lanes] * inv[:, lanes]).T
        o_ref[:, cols] = (oh * _silu(gc_ref[:, cols].astype(F32))).astype(o_ref.dtype)


def _attention(p_q, cos, sin, q_g, k_ctx, vt_ctx, k_lat, vt_lat, batch, tq, tk):
    T = p_q.shape[0]
    S = T // batch
    nq = S // tq
    n_ctx = k_ctx.shape[2]
    n_lat_chunks = 0 if k_lat is None else k_lat.shape[2] // tk
    assert n_lat_chunks % 2 == 0
    gw = GQA_REP * HEAD_DIM
    in_specs = [pl.BlockSpec((tq, gw), lambda b, g, i: (b * nq + i, OFF_Q // gw + g)),
                pl.BlockSpec((tq, gw), lambda b, g, i: (b * nq + i, OFF_GC // gw + g)),
                pl.BlockSpec((tq, HEAD_DIM), lambda b, g, i: (i, 0)),
                pl.BlockSpec((tq, HEAD_DIM), lambda b, g, i: (i, 0)),
                pl.BlockSpec((1, HEAD_DIM), lambda b, g, i: (0, 0)),
                pl.BlockSpec((None, None, n_ctx, HEAD_DIM), lambda b, g, i: (b, g, 0, 0)),
                pl.BlockSpec((None, None, VT_ROWS, n_ctx), lambda b, g, i: (b, g, 0, 0))]
    args = [p_q, p_q, cos, sin, q_g, k_ctx, vt_ctx]
    scratch = [pltpu.VMEM((HEAD_DIM, GQA_REP * tq), BF16), pltpu.VMEM((VT_ROWS, GQA_REP * tq), F32)]
    if n_lat_chunks:
        n_lat = k_lat.shape[2]
        in_specs += [pl.BlockSpec((None, None, n_lat, HEAD_DIM), lambda b, g, i: (b, g, 0, 0)),
                     pl.BlockSpec((None, None, VT_ROWS, n_lat), lambda b, g, i: (b, g, 0, 0))]
        args += [k_lat, vt_lat]
        scratch += [pltpu.VMEM((tk, GQA_REP * tq), F32), pltpu.VMEM((tk, GQA_REP * tq), F32)]
    return pl.pallas_call(
        functools.partial(_attn_kernel, tq=tq, tk=tk, n_lat_chunks=n_lat_chunks),
        out_shape=jax.ShapeDtypeStruct((T, D_MODEL), BF16),
        grid=(batch, N_KV_HEADS, nq),
        in_specs=in_specs,
        out_specs=pl.BlockSpec((tq, gw), lambda b, g, i: (b * nq + i, g)),
        scratch_shapes=scratch,
        compiler_params=_cparams(("arbitrary", "arbitrary", "arbitrary")),
        name="gqa_flash_lat" if n_lat_chunks else "gqa_flash_ctx",
    )(*args)


def _merge_kernel(u_ref, v_ref, ga_ref, sw_ref, sbias_ref, sgain_ref, sf_ref, sb_ref, z_ref, yc_ref, gm_ref,
                  x_ref, mod_ref, wb_ref, wo_ref, bo_ref, sg_ref, lg_ref, lb_ref, out_ref, ya_ref):
    _sgu_kernel(u_ref, v_ref, ga_ref, sw_ref, sbias_ref, sgain_ref, ya_ref, n_chunks=u_ref.shape[0] // CHUNK)
    s = (sf_ref[...].astype(F32) + sb_ref[...].astype(F32)) * _silu(z_ref[...].astype(F32))
    yb = s * lax.rsqrt(jnp.mean(s * s, axis=-1, keepdims=True) + RMS_EPS) * sg_ref[...]
    gate_of = lambda k: jax.nn.sigmoid(gm_ref[:, k * D_MODEL:(k + 1) * D_MODEL].astype(F32))
    merged = gate_of(0) * jnp.dot(ya_ref[...], wb_ref[0], preferred_element_type=F32)
    merged = merged + gate_of(1) * jnp.dot(yb.astype(BF16), wb_ref[1], preferred_element_type=F32)
    merged = merged + gate_of(2) * jnp.dot(yc_ref[...], wb_ref[2], preferred_element_type=F32)
    out = jnp.dot(merged.astype(BF16), wo_ref[...], preferred_element_type=F32) + bo_ref[...]
    gate = mod_ref[:, 2 * D_MODEL:3 * D_MODEL]
    r = DEEPNORM_ALPHA * x_ref[...] + gate * out
    mu = jnp.mean(r, axis=-1, keepdims=True)
    rc = r - mu
    var = jnp.mean(rc * rc, axis=-1, keepdims=True)
    out_ref[...] = rc * lax.rsqrt(var + LN_EPS) * lg_ref[...] + lb_ref[...]


def _merge(sgu_w, sgu_bfull, sgu_g, s_f, s_b, p, y_c, x2d, mod, w_branch, w_out, b_out, ssm_g, ln_g, ln_b,
           tm, mod_row):
    T = x2d.shape[0]
    row = lambda: pl.BlockSpec((tm, D_MODEL), lambda i: (i, 0))
    pcol = lambda off: pl.BlockSpec((tm, D_MODEL), lambda i: (i, off // D_MODEL))
    vec = lambda: pl.BlockSpec((1, D_MODEL), lambda i: (0, 0))
    once = pl.Buffered(1)
    return pl.pallas_call(
        _merge_kernel,
        out_shape=jax.ShapeDtypeStruct((T, D_MODEL), F32),
        grid=(T // tm,),
        in_specs=[pcol(OFF_U), pcol(OFF_V), pcol(OFF_GA),
                  pl.BlockSpec((A_GROUPS, CHUNK, CHUNK), lambda i: (0, 0, 0)),
                  pl.BlockSpec((CHUNK, D_MODEL), lambda i: (0, 0)),
                  vec(),
                  row(), row(), pcol(OFF_Z), row(),
                  pl.BlockSpec((tm, 3 * D_MODEL), lambda i: (i, OFF_GM // (3 * D_MODEL))),
                  row(),
                  pl.BlockSpec((None, 1, 3 * D_MODEL), lambda i: (mod_row(i), 0, 0)),
                  pl.BlockSpec((3, D_MODEL, D_MODEL), lambda i: (0, 0, 0), pipeline_mode=once),
                  pl.BlockSpec((D_MODEL, D_MODEL), lambda i: (0, 0), pipeline_mode=once),
                  vec(), vec(), vec(), vec()],
        out_specs=row(),
        scratch_shapes=[pltpu.VMEM((tm, D_MODEL), BF16)],
        compiler_params=_cparams(("arbitrary",)),
        name="sgu_merge_out_deepnorm",
    )(p, p, p, sgu_w, sgu_bfull, sgu_g, s_f, s_b, p, y_c, p, x2d, mod, w_branch, w_out, b_out, ssm_g, ln_g, ln_b)


def _pack_segments():
    ref_off, o = {}, 0
    for name, n in (("u", 1024), ("v", 1024), ("ga", 1024), ("z", 1024), ("xbc", CONV_DIM),
                    ("dt", 2 * SSM_HEADS), ("q", 1024), ("k", KV_WIDTH), ("vv", KV_WIDTH),
                    ("gc", 1024), ("gm", 3 * D_MODEL)):
        ref_off[name] = (o, n)
        o += n
    packed = (("u", OFF_U), ("v", OFF_V), ("ga", OFF_GA), ("z", OFF_Z), ("q", OFF_Q), ("gc", OFF_GC),
              ("gm", OFF_GM), ("xbc", OFF_XBC), ("k", OFF_K), ("vv", OFF_VV), ("dt", OFF_DT))
    return tuple((dst,) + ref_off[name] for name, dst in packed), o


PACK_SEGMENTS, N_IN = _pack_segments()


def _pack_kernel(w_ref, o_ref):
    for dst, src, n in PACK_SEGMENTS:
        o_ref[dst:dst + n, :] = w_ref[src:src + n, :].astype(BF16)
    used = OFF_DT + 2 * SSM_HEADS
    o_ref[used:, :] = jnp.zeros((N_PACK - used, o_ref.shape[1]), BF16)


def _pack_w_in(w_in_t, tl=256):
    L, n_in, d = w_in_t.shape
    assert n_in == N_IN
    return pl.pallas_call(
        _pack_kernel,
        out_shape=jax.ShapeDtypeStruct((L, N_PACK, d), BF16),
        grid=(L, d // tl),
        in_specs=[pl.BlockSpec((None, n_in, tl), lambda l, i: (l, 0, i))],
        out_specs=pl.BlockSpec((None, N_PACK, tl), lambda l, i: (l, 0, i)),
        compiler_params=_cparams(("arbitrary", "arbitrary")),
        name="pack_w_in",
    )(w_in_t)


def _rope_tables(n_tokens):
    rows = n_tokens // GRID_W
    row = np.repeat(np.arange(rows, dtype=np.float32), GRID_W)
    col = np.tile(np.arange(GRID_W, dtype=np.float32), rows)
    inv = np.float32(ROPE_THETA) ** (-np.arange(ROPE_PAIRS_PER_AXIS, dtype=np.float32) / ROPE_PAIRS_PER_AXIS)
    ang = np.concatenate([row[:, None] * inv, col[:, None] * inv], axis=-1).astype(np.float32)
    cos, sin = np.cos(ang), np.sin(ang)
    cos_f = np.stack([cos, cos], axis=-1).reshape(n_tokens, HEAD_DIM)
    sin_f = np.stack([-sin, sin], axis=-1).reshape(n_tokens, HEAD_DIM)
    return jnp.asarray(cos_f, F32), jnp.asarray(sin_f, F32)


def _pad_lanes(v, n):
    return jnp.concatenate([v, jnp.zeros((n - v.shape[0],), v.dtype)])[None, :]


def kernel(x, c, ctx, c_ctx, w_mod, b_mod, w_in, sgu_w, sgu_b, sgu_norm_g, conv_w, conv_b, dt_bias,
           a_log, d_skip, ssm_norm_g, q_norm_g, k_norm_g, w_branch, w_out, b_out, ln_g, ln_b):
    batch, seq, _ = x.shape
    n_ctx = ctx.shape[1]
    assert n_ctx == N_CTX_CHUNKS * CHUNK and seq % 1024 == 0

    cos_l, sin_l = _rope_tables(seq)
    cos_c = jnp.ones((n_ctx, HEAD_DIM), F32)
    sin_c = jnp.zeros((n_ctx, HEAD_DIM), F32)

    c_rows = jnp.concatenate([c, c_ctx[None, :], jnp.zeros((8 - batch - 1, D_MODEL), F32)], axis=0)
    mod_all = _modulation(c_rows, w_mod, b_mod[:, None, :])
    ctx_row = batch

    hid = jnp.arange(CHUNK)[:, None]
    ch = jnp.arange(D_MODEL)[None, :] // SSM_HEAD_DIM
    e_mats = jnp.stack([(hid == ch + d * SSM_HEADS).astype(BF16) for d in range(2)])

    lat_per_tile = 1024
    tiles_per_batch = seq // lat_per_tile
    xl = x.reshape(batch * seq, D_MODEL)
    xc = ctx.reshape(batch * n_ctx, D_MODEL)

    w_pack_all = _pack_w_in(jnp.swapaxes(w_in, 1, 2))
    shift_mat = _conv_shift_matrix()

    for l in range(DEPTH):
        ctx_out = l < DEPTH - 1
        mod = mod_all[l][:, None, :]
        wb = w_branch[l].astype(BF16)
        wo = w_out[l].astype(BF16)
        q_g, k_g = q_norm_g[l][None, :], k_norm_g[l][None, :]
        sgu_wb = sgu_w[l].astype(BF16)
        sgu_bfull = jnp.repeat(sgu_b[l].T, A_GROUP_DIM, axis=1)
        sgu_g = sgu_norm_g[l][None, :]
        dtb = _pad_lanes(dt_bias[l].reshape(-1), CHUNK)
        alog = _pad_lanes(a_log[l].reshape(-1), CHUNK)

        lat_row = lambda i: i // tiles_per_batch
        p_lat, dt_lat = _inproj(xl, mod, w_pack_all, l, lat_per_tile, lat_row)
        p_ctx, dt_ctx = _inproj(xc, mod, w_pack_all, l, n_ctx, lambda i: ctx_row)

        xconv_lat = _conv(p_lat, shift_mat, conv_w[l], conv_b[l][None, :], seq, 512)
        xconv_ctx = _conv(p_ctx, shift_mat, conv_w[l], conv_b[l][None, :], n_ctx, n_ctx)
        dskip_x = jnp.repeat(d_skip[l], SSM_HEAD_DIM, axis=1)[:, None, :]
        ssd = _ssd(xconv_ctx, xconv_lat, dt_ctx, dt_lat, dtb, alog, dskip_x, e_mats, batch)

        k_ctx, vt_ctx = _kvprep(p_ctx, cos_c, sin_c, k_g, batch, n_ctx)
        k_lat, vt_lat = _kvprep(p_lat, cos_l, sin_l, k_g, batch, 512)
        o_lat = _attention(p_lat, cos_l, sin_l, q_g, k_ctx, vt_ctx, k_lat, vt_lat, batch, 512, 1024)

        sgu = (sgu_wb, sgu_bfull, sgu_g)
        vecs = (b_out[l][None, :], ssm_norm_g[l][None, :], ln_g[l][None, :], ln_b[l][None, :])
        merge_tm = 512
        merge_tiles_per_batch = seq // merge_tm
        xl_new = _merge(*sgu, ssd[0][1], ssd[1][1], p_lat, o_lat, xl, mod, wb, wo, *vecs,
                        merge_tm, lambda i: i // merge_tiles_per_batch)
        if ctx_out:
            o_ctx = _attention(p_ctx, cos_c, sin_c, q_g, k_ctx, vt_ctx, None, None, batch, n_ctx, 0)
            xc = _merge(*sgu, ssd[0][0], ssd[1][0], p_ctx, o_ctx, xc, mod, wb, wo, *vecs,
                        n_ctx, lambda i: ctx_row)
        xl = xl_new

    return xl.reshape(batch, seq, D_MODEL)
```

```python
import functools
import math

import jax
import jax.numpy as jnp
import numpy as np
from jax import lax
from jax.experimental import pallas as pl
from jax.experimental.pallas import tpu as pltpu

D_MODEL = 1024
DEPTH = 2
GRID_W = 64
CHUNK = 128
A_GROUPS = 4
A_GROUP_DIM = D_MODEL // A_GROUPS
SSM_HEAD_DIM = 64
SSM_HEADS = 16
SSM_GROUPS = 2
SSM_STATE = 128
D_CONV = 5
CONV_DIM = D_MODEL + 2 * SSM_GROUPS * SSM_STATE
HEAD_DIM = 128
N_Q_HEADS = 8
N_KV_HEADS = 2
GQA_REP = N_Q_HEADS // N_KV_HEADS
KV_WIDTH = N_KV_HEADS * HEAD_DIM
ROPE_THETA = 10000.0
ROPE_PAIRS_PER_AXIS = HEAD_DIM // 4
DEEPNORM_ALPHA = (2 * DEPTH) ** 0.25
LN_EPS = 1e-6
RMS_EPS = 1e-6

OFF_U, OFF_V, OFF_GA, OFF_Z, OFF_Q, OFF_GC, OFF_GM = 0, 1024, 2048, 3072, 4096, 5120, 6144
OFF_XBC = 9216
OFF_K = OFF_XBC + CONV_DIM
OFF_VV = OFF_K + KV_WIDTH
OFF_DT = OFF_VV + KV_WIDTH
DT_PAD = 256
N_PACK = OFF_DT + DT_PAD
PROJ_TN = 1280

BF16_SUBLANES = 16
VT_ROWS = HEAD_DIM + BF16_SUBLANES

VMEM_LIMIT = 56 * 1024 * 1024
F32 = jnp.float32
BF16 = jnp.bfloat16
NEG_BIG = -1e30


def _cparams(sem):
    return pltpu.CompilerParams(dimension_semantics=sem, vmem_limit_bytes=VMEM_LIMIT)


def _silu(t):
    return t * jax.nn.sigmoid(t)


def _split_dot(x, e, terms=3):
    acc = None
    r = x
    for t in range(terms):
        xt = r.astype(BF16)
        if t + 1 < terms:
            r = r - xt.astype(F32)
        part = jnp.dot(xt, e, preferred_element_type=F32)
        acc = part if acc is None else acc + part
    return acc


def _split_dot_rhs(e, x):
    x1 = x.astype(BF16)
    r1 = x - x1.astype(F32)
    x2 = r1.astype(BF16)
    r2 = r1 - x2.astype(F32)
    x3 = r2.astype(BF16)
    acc = jnp.dot(e, x1, preferred_element_type=F32)
    acc = acc + jnp.dot(e, x2, preferred_element_type=F32)
    return acc + jnp.dot(e, x3, preferred_element_type=F32)


def _mod_kernel(c_ref, w_ref, b_ref, o_ref):
    c = c_ref[...]
    act = _silu(c)
    o_ref[...] = jnp.dot(act, w_ref[...], preferred_element_type=F32,
                         precision=lax.Precision.HIGHEST) + b_ref[...]


def _modulation(c_rows, w_mod, b_mod):
    L = w_mod.shape[0]
    tn = 512
    return pl.pallas_call(
        _mod_kernel,
        out_shape=jax.ShapeDtypeStruct((L, 8, 3 * D_MODEL), F32),
        grid=(L, 3 * D_MODEL // tn),
        in_specs=[pl.BlockSpec((8, D_MODEL), lambda l, j: (0, 0)),
                  pl.BlockSpec((None, D_MODEL, tn), lambda l, j: (l, 0, j)),
                  pl.BlockSpec((None, 1, tn), lambda l, j: (l, 0, j))],
        out_specs=pl.BlockSpec((None, 8, tn), lambda l, j: (l, 0, j)),
        compiler_params=_cparams(("arbitrary", "arbitrary")),
        name="adaln_modulation",
    )(c_rows, w_mod, b_mod)


def _inproj_kernel(x_ref, mod_ref, w_ref, o_ref, dt_ref, h_ref):
    j = pl.program_id(1)

    @pl.when(j == 0)
    def _():
        x = x_ref[...]
        mu = jnp.mean(x, axis=-1, keepdims=True)
        xc = x - mu
        var = jnp.mean(xc * xc, axis=-1, keepdims=True)
        shift = mod_ref[:, 0:D_MODEL]
        scale = mod_ref[:, D_MODEL:2 * D_MODEL]
        h = xc * lax.rsqrt(var + LN_EPS) * (1.0 + scale) + shift
        h_ref[...] = h.astype(BF16)

    acc = lax.dot_general(h_ref[...], w_ref[...], (((1,), (1,)), ((), ())), preferred_element_type=F32)
    o_ref[...] = acc.astype(o_ref.dtype)

    @pl.when(j == pl.num_programs(1) - 1)
    def _():
        dt_ref[...] = acc[:, PROJ_TN - DT_PAD:]


def _inproj(x2d, mod, w_pack, layer, tm, mod_row):
    T = x2d.shape[0]
    assert OFF_DT + DT_PAD == N_PACK and N_PACK % PROJ_TN == 0
    return pl.pallas_call(
        _inproj_kernel,
        out_shape=(jax.ShapeDtypeStruct((T, N_PACK), BF16),
                   jax.ShapeDtypeStruct((T, DT_PAD), F32)),
        grid=(T // tm, N_PACK // PROJ_TN),
        in_specs=[pl.BlockSpec((tm, D_MODEL), lambda i, j: (i, 0)),
                  pl.BlockSpec((None, 1, 3 * D_MODEL), lambda i, j: (mod_row(i), 0, 0)),
                  pl.BlockSpec((None, PROJ_TN, D_MODEL), lambda i, j: (layer, j, 0))],
        out_specs=(pl.BlockSpec((tm, PROJ_TN), lambda i, j: (i, j)),
                   pl.BlockSpec((tm, DT_PAD), lambda i, j: (i, 0))),
        scratch_shapes=[pltpu.VMEM((tm, D_MODEL), BF16)],
        compiler_params=_cparams(("arbitrary", "arbitrary")),
        name="ln_mod_inproj",
    )(x2d, mod, w_pack)


def _sgu_kernel(u_ref, v_ref, ga_ref, w_ref, b_ref, g_ref, o_ref, *, n_chunks):
    for c in range(n_chunks):
        rows = slice(c * CHUNK, (c + 1) * CHUNK)
        for g in range(A_GROUPS):
            cols = slice(g * A_GROUP_DIM, (g + 1) * A_GROUP_DIM)
            v = v_ref[rows, cols].astype(F32)
            mu = jnp.mean(v, axis=-1, keepdims=True)
            vc = v - mu
            var = jnp.mean(vc * vc, axis=-1, keepdims=True)
            vn = vc * lax.rsqrt(var + LN_EPS) * g_ref[:, cols]
            mixed = jnp.dot(w_ref[g], vn.astype(BF16), preferred_element_type=F32)
            mixed = mixed + b_ref[:, cols]
            y = u_ref[rows, cols].astype(F32) * mixed * _silu(ga_ref[rows, cols].astype(F32))
            o_ref[rows, cols] = y.astype(o_ref.dtype)


CONV_HALO = 64
CONV_WINDOW = CHUNK + 2 * CONV_HALO


def _conv_shift_matrix():
    pad = (D_CONV - 1) // 2
    m = np.zeros((D_CONV * CHUNK, CONV_WINDOW), np.float32)
    for j in range(D_CONV):
        t = np.arange(CHUNK)
        m[j * CHUNK + t, t + CONV_HALO + j - pad] = 1.0
    return jnp.asarray(m, BF16)


def _conv_kernel(prev_ref, cur_ref, next_ref, shift_ref, w_ref, b_ref, o_ref, ext_ref, *, tiles_per_seq, tc):
    i = pl.program_id(0)
    pos = i % tiles_per_seq
    zeros = jnp.zeros((CONV_HALO, CONV_DIM), BF16)
    ext_ref[0:CONV_HALO, :] = jnp.where(pos == 0, zeros, prev_ref[...])
    ext_ref[CONV_HALO:CONV_HALO + tc, :] = cur_ref[...]
    ext_ref[CONV_HALO + tc:, :] = jnp.where(pos == tiles_per_seq - 1, zeros, next_ref[...])
    for k in range(tc // CHUNK):
        window = ext_ref[k * CHUNK:k * CHUNK + CONV_WINDOW, :]
        taps = jnp.dot(shift_ref[...], window, preferred_element_type=F32)
        acc = b_ref[...] + taps[0:CHUNK, :] * w_ref[0:1, :]
        for j in range(1, D_CONV):
            acc = acc + taps[j * CHUNK:(j + 1) * CHUNK, :] * w_ref[j:j + 1, :]
        o_ref[k * CHUNK:(k + 1) * CHUNK, :] = _silu(acc).astype(o_ref.dtype)


def _conv(p, shift_mat, conv_w, conv_b, seq_len, tc):
    T = p.shape[0]
    n_tiles = T // tc
    hb = tc // CONV_HALO
    n_hblocks = T // CONV_HALO
    colb = OFF_XBC // CONV_DIM
    return pl.pallas_call(
        functools.partial(_conv_kernel, tiles_per_seq=seq_len // tc, tc=tc),
        out_shape=jax.ShapeDtypeStruct((T, CONV_DIM), BF16),
        grid=(n_tiles,),
        in_specs=[pl.BlockSpec((CONV_HALO, CONV_DIM), lambda i: (jnp.maximum(i * hb - 1, 0), colb)),
                  pl.BlockSpec((tc, CONV_DIM), lambda i: (i, colb)),
                  pl.BlockSpec((CONV_HALO, CONV_DIM),
                               lambda i: (jnp.minimum((i + 1) * hb, n_hblocks - 1), colb)),
                  pl.BlockSpec((D_CONV * CHUNK, CONV_WINDOW), lambda i: (0, 0)),
                  pl.BlockSpec((D_CONV, CONV_DIM), lambda i: (0, 0)),
                  pl.BlockSpec((1, CONV_DIM), lambda i: (0, 0))],
        out_specs=pl.BlockSpec((tc, CONV_DIM), lambda i: (i, 0)),
        scratch_shapes=[pltpu.VMEM((tc + 2 * CONV_HALO, CONV_DIM), BF16)],
        compiler_params=_cparams(("arbitrary",)),
        name="dwconv_silu",
    )(p, p, p, shift_mat, conv_w, conv_b)


N_CTX_CHUNKS = 2


SSD_BLOCK = 512


def _ssd_scan(dt_raw, dtb, alog, d):
    reverse = d == 1
    dt = jax.nn.softplus(dt_raw + dtb)
    a = dt * (-jnp.exp(alog) * LOG2E)

    row = lax.broadcasted_iota(jnp.int32, (CHUNK, CHUNK), 0)
    col = lax.broadcasted_iota(jnp.int32, (CHUNK, CHUNK), 1)
    mask = (row <= col) if reverse else (row >= col)
    tri = mask.astype(BF16)
    tri_t = ((row >= col) if reverse else (row <= col)).astype(BF16)

    a_cs = _split_dot_rhs(tri, a)
    a_cs_t = _split_dot(a.T, tri_t)
    return dt, a_cs, a_cs_t, mask


def _ssd_group(g, scan, x_cols, dskip_ref, e_ref, ht_ref, y_ref, d):
    dt, a_cs, a_cs_t, mask = scan
    end = 0 if d == 1 else CHUNK - 1
    lane = lax.broadcasted_iota(jnp.int32, (CHUNK, 2 * SSM_HEAD_DIM), 1)
    left = lane < SSM_HEAD_DIM
    gw = D_MODEL // SSM_GROUPS
    heads_per_group = SSM_HEADS // SSM_GROUPS
    heads_per_block = SSD_BLOCK // SSM_HEAD_DIM
    b_g = x_cols(D_MODEL + g * SSM_STATE, D_MODEL + (g + 1) * SSM_STATE)
    c_g = x_cols(D_MODEL + (SSM_GROUPS + g) * SSM_STATE, D_MODEL + (SSM_GROUPS + g + 1) * SSM_STATE)
    cb = lax.dot_general(c_g, b_g, (((1,), (1,)), ((), ())), preferred_element_type=F32)
    b_t = b_g.astype(F32).T.astype(BF16)
    for blk in range(gw // SSD_BLOCK):
        c0 = g * gw + blk * SSD_BLOCK
        cols = slice(c0, c0 + SSD_BLOCK)
        gcols = slice(blk * SSD_BLOCK, (blk + 1) * SSD_BLOCK)
        e_blk = e_ref[:, cols]
        a_cs_x = _split_dot(a_cs, e_blk, terms=2)
        dt_x = _split_dot(dt, e_blk, terms=1)
        a_tot_x = a_cs_x[end:end + 1, :]
        xs = x_cols(c0, c0 + SSD_BLOCK).astype(F32)
        xdt = xs * dt_x
        xdt_bf = xdt.astype(BF16)
        xw = (xdt * jnp.exp2(a_tot_x - a_cs_x)).astype(BF16)
        h_in = ht_ref[g, :, gcols]
        y_blk = jnp.dot(c_g, h_in.astype(BF16), preferred_element_type=F32) * jnp.exp2(a_cs_x)
        ht_ref[g, :, gcols] = h_in * jnp.exp2(a_tot_x) + jnp.dot(b_t, xw, preferred_element_type=F32)
        y_pairs = []
        for j in range(heads_per_block // 2):
            ms = []
            for k in range(2):
                hh = d * SSM_HEADS + g * heads_per_group + blk * heads_per_block + 2 * j + k
                seg = a_cs[:, hh:hh + 1] - a_cs_t[hh:hh + 1, :]
                dec = jnp.exp2(jnp.where(mask, seg, NEG_BIG))
                ms.append((cb * dec).astype(BF16))
            lhs = jnp.concatenate(ms, axis=1)
            xp = xdt_bf[:, 2 * j * SSM_HEAD_DIM:(2 * j + 2) * SSM_HEAD_DIM]
            zero = jnp.zeros_like(xp)
            rhs = jnp.concatenate([jnp.where(left, xp, zero), jnp.where(left, zero, xp)], axis=0)
            y_pairs.append(jnp.dot(lhs, rhs, preferred_element_type=F32))
        y_blk = y_blk + jnp.concatenate(y_pairs, axis=1) + dskip_ref[:, cols] * xs
        y_ref[:, cols] = y_blk.astype(y_ref.dtype)


def _ssd_kernel(xcf_ref, xlf_ref, dtcf_ref, dtlf_ref, xcr_ref, xlr_ref, dtcr_ref, dtlr_ref,
                dtb_ref, alog_ref, dskip_ref, e_ref,
                ycf_ref, ylf_ref, ycr_ref, ylr_ref, ht_ref, y_ref):
    s = pl.program_id(0)
    is_ctx = s < N_CTX_CHUNKS
    batch = xcf_ref.shape[0]

    @pl.when(s == 0)
    def _():
        ht_ref[...] = jnp.zeros_like(ht_ref)

    sides = ((xcf_ref, xlf_ref, dtcf_ref, dtlf_ref, ycf_ref, ylf_ref),
             (xcr_ref, xlr_ref, dtcr_ref, dtlr_ref, ycr_ref, ylr_ref))
    chains = []
    for b in range(batch):
        for d, (xc_ref, xl_ref, dtc_ref, dtl_ref, _, _) in enumerate(sides):
            x_cols = lambda c0, c1, b=b, xc_ref=xc_ref, xl_ref=xl_ref: jnp.where(
                is_ctx, xc_ref[b, :, c0:c1], xl_ref[b, :, c0:c1])
            dt_raw = jnp.where(is_ctx, dtc_ref[b, :, 0:CHUNK], dtl_ref[b, :, 0:CHUNK])
            chains.append((b, d, x_cols, _ssd_scan(dt_raw, dtb_ref[...], alog_ref[...], d)))
    for g in range(SSM_GROUPS):
        for b, d, x_cols, scan in chains:
            _ssd_group(g, scan, x_cols, dskip_ref.at[d], e_ref.at[d], ht_ref.at[b, d], y_ref.at[b, d], d)

    @pl.when(is_ctx)
    def _():
        for d in range(2):
            sides[d][4][...] = y_ref[:, d]

    @pl.when(jnp.logical_not(is_ctx))
    def _():
        for d in range(2):
            sides[d][5][...] = y_ref[:, d]


def _ssd(xconv_ctx, xconv_lat, dt_ctx, dt_lat, dtb, alog, dskip_x, e_mats, batch):
    n_lat = xconv_lat.shape[0] // batch // CHUNK
    n_steps = N_CTX_CHUNKS + n_lat
    per_batch = lambda t: t.reshape(batch, t.shape[0] // batch, t.shape[1])

    cidx = (lambda s: jnp.minimum(s, N_CTX_CHUNKS - 1),
            lambda s: jnp.maximum(N_CTX_CHUNKS - 1 - s, 0))
    lidx = (lambda s: jnp.maximum(s - N_CTX_CHUNKS, 0),
            lambda s: jnp.minimum(n_steps - 1 - s, n_lat - 1))

    def specs(width, d):
        return [pl.BlockSpec((batch, CHUNK, width), lambda s, d=d: (0, cidx[d](s), 0)),
                pl.BlockSpec((batch, CHUNK, width), lambda s, d=d: (0, lidx[d](s), 0))]

    const = lambda shape: pl.BlockSpec(shape, lambda s: (0,) * len(shape))
    in_specs, out_specs, args = [], [], []
    for d in range(2):
        in_specs += specs(CONV_DIM, d) + specs(DT_PAD, d)
        args += [per_batch(xconv_ctx), per_batch(xconv_lat), per_batch(dt_ctx), per_batch(dt_lat)]
        out_specs += specs(D_MODEL, d)
    in_specs += [const((1, CHUNK)), const((1, CHUNK)), const((2, 1, D_MODEL)), const((2, CHUNK, D_MODEL))]
    y_shapes = (jax.ShapeDtypeStruct((batch, xconv_ctx.shape[0] // batch, D_MODEL), BF16),
                jax.ShapeDtypeStruct((batch, xconv_lat.shape[0] // batch, D_MODEL), BF16))
    ys = pl.pallas_call(
        _ssd_kernel,
        out_shape=y_shapes + y_shapes,
        grid=(n_steps,),
        in_specs=in_specs,
        out_specs=tuple(out_specs),
        scratch_shapes=[pltpu.VMEM((batch, 2, SSM_GROUPS, SSM_STATE, D_MODEL // SSM_GROUPS), F32),
                        pltpu.VMEM((batch, 2, CHUNK, D_MODEL), BF16)],
        compiler_params=_cparams(("arbitrary",)),
        name="ssd_scan_bidir",
    )(*args, dtb, alog, dskip_x, e_mats)
    ycf, ylf, ycr, ylr = (y.reshape(-1, D_MODEL) for y in ys)
    return (ycf, ylf), (ycr, ylr)


def _norm_rope(t, g, cos, sin):
    tn = t * lax.rsqrt(jnp.mean(t * t, axis=-1, keepdims=True) + RMS_EPS) * g
    lane = lax.broadcasted_iota(jnp.int32, tn.shape, 1)
    partner = jnp.where(lane % 2 == 0, pltpu.roll(tn, HEAD_DIM - 1, axis=1), pltpu.roll(tn, 1, axis=1))
    return tn * cos + partner * sin


def _kvprep_kernel(k_ref, v_ref, cos_ref, sin_ref, g_ref, ko_ref, vto_ref):
    for h in range(N_KV_HEADS):
        cols = slice(h * HEAD_DIM, (h + 1) * HEAD_DIM)
        kh = _norm_rope(k_ref[:, cols].astype(F32), g_ref[...], cos_ref[...], sin_ref[...])
        ko_ref[h] = kh.astype(BF16)
        vto_ref[h, 0:HEAD_DIM, :] = v_ref[:, cols].astype(F32).T.astype(BF16)
        vto_ref[h, HEAD_DIM:, :] = jnp.ones((VT_ROWS - HEAD_DIM, k_ref.shape[0]), BF16)


def _kvprep(p, cos, sin, k_g, batch, tm):
    T = p.shape[0]
    S = T // batch
    nt = S // tm
    return pl.pallas_call(
        _kvprep_kernel,
        out_shape=(jax.ShapeDtypeStruct((batch, N_KV_HEADS, S, HEAD_DIM), BF16),
                   jax.ShapeDtypeStruct((batch, N_KV_HEADS, VT_ROWS, S), BF16)),
        grid=(batch, nt),
        in_specs=[pl.BlockSpec((tm, KV_WIDTH), lambda b, i: (b * nt + i, OFF_K // KV_WIDTH)),
                  pl.BlockSpec((tm, KV_WIDTH), lambda b, i: (b * nt + i, OFF_VV // KV_WIDTH)),
                  pl.BlockSpec((tm, HEAD_DIM), lambda b, i: (i, 0)),
                  pl.BlockSpec((tm, HEAD_DIM), lambda b, i: (i, 0)),
                  pl.BlockSpec((1, HEAD_DIM), lambda b, i: (0, 0))],
        out_specs=(pl.BlockSpec((None, N_KV_HEADS, tm, HEAD_DIM), lambda b, i: (b, 0, i, 0)),
                   pl.BlockSpec((None, N_KV_HEADS, VT_ROWS, tm), lambda b, i: (b, 0, 0, i))),
        compiler_params=_cparams(("arbitrary", "arbitrary")),
        name="kv_prep",
    )(p, p, cos, sin, k_g)


LOG2E = 1.4426950408889634


def _attn_kernel(*refs, tq, tk, n_lat_chunks):
    if n_lat_chunks:
        (q_ref, gc_ref, cos_ref, sin_ref, g_ref, kc_ref, vtc_ref, kl_ref, vtl_ref, o_ref,
         qt_ref, acc_ref, s0_ref, s1_ref) = refs
    else:
        q_ref, gc_ref, cos_ref, sin_ref, g_ref, kc_ref, vtc_ref, o_ref, qt_ref, acc_ref = refs
    scale = HEAD_DIM ** -0.5 * LOG2E
    for h in range(GQA_REP):
        cols = slice(h * HEAD_DIM, (h + 1) * HEAD_DIM)
        qh = _norm_rope(q_ref[:, cols].astype(F32), g_ref[...], cos_ref[...], sin_ref[...]) * scale
        qt_ref[:, h * tq:(h + 1) * tq] = qh.T.astype(BF16)

    def scores(k):
        return jnp.dot(k, qt_ref[...], preferred_element_type=F32)

    st = scores(kc_ref[...])
    m = jnp.max(st, axis=0, keepdims=True)
    p = jnp.exp2(st - m)
    acc_ref[...] = jnp.dot(vtc_ref[...], p.astype(BF16), preferred_element_type=F32)

    if n_lat_chunks:
        def offset(j):
            return j * tk if isinstance(j, int) else pl.multiple_of(j * tk, tk)

        def k_chunk(j):
            return kl_ref[pl.ds(offset(j), tk), :]

        def vt_chunk(j):
            return vtl_ref[:, pl.ds(offset(j), tk)]

        def scores_into(s_ref, k):
            st = scores(k)
            s_ref[...] = st
            return jnp.max(st, axis=0, keepdims=True)

        def update(s_ref, vt, m, cmax):
            m_new = jnp.maximum(m, cmax)
            alpha = jnp.exp2(m - m_new)
            p = jnp.exp2(s_ref[...] - m_new)
            acc_ref[...] = alpha * acc_ref[...] + jnp.dot(vt, p.astype(BF16), preferred_element_type=F32)
            return m_new

        cm0 = scores_into(s0_ref, k_chunk(0))

        def pair(jj, carry):
            m, cm0 = carry
            j = 2 * jj
            cm1 = scores_into(s1_ref, k_chunk(j + 1))
            m = update(s0_ref, vt_chunk(j), m, cm0)
            cm0 = scores_into(s0_ref, k_chunk(j + 2))
            m = update(s1_ref, vt_chunk(j + 1), m, cm1)
            return m, cm0

        m, cm0 = lax.fori_loop(0, n_lat_chunks // 2 - 1, pair, (m, cm0))
        j = n_lat_chunks - 2
        cm1 = scores_into(s1_ref, k_chunk(j + 1))
        m = update(s0_ref, vt_chunk(j), m, cm0)
        m = update(s1_ref, vt_chunk(j + 1), m, cm1)

    inv = 1.0 / acc_ref[HEAD_DIM:HEAD_DIM + 1, :]
    for h in range(GQA_REP):
        cols = slice(h * HEAD_DIM, (h + 1) * HEAD_DIM)
        lanes = slice(h * tq, (h + 1) * tq)
        oh = (acc_ref[0:HEAD_DIM, lanes] * inv[:, lanes]).T
        o_ref[:, cols] = (oh * _silu(gc_ref[:, cols].astype(F32))).astype(o_ref.dtype)


def _attention(p_q, cos, sin, q_g, k_ctx, vt_ctx, k_lat, vt_lat, batch, tq, tk):
    T = p_q.shape[0]
    S = T // batch
    nq = S // tq
    n_ctx = k_ctx.shape[2]
    n_lat_chunks = 0 if k_lat is None else k_lat.shape[2] // tk
    assert n_lat_chunks % 2 == 0
    gw = GQA_REP * HEAD_DIM
    in_specs = [pl.BlockSpec((tq, gw), lambda b, g, i: (b * nq + i, OFF_Q // gw + g)),
                pl.BlockSpec((tq, gw), lambda b, g, i: (b * nq + i, OFF_GC // gw + g)),
                pl.BlockSpec((tq, HEAD_DIM), lambda b, g, i: (i, 0)),
                pl.BlockSpec((tq, HEAD_DIM), lambda b, g, i: (i, 0)),
                pl.BlockSpec((1, HEAD_DIM), lambda b, g, i: (0, 0)),
                pl.BlockSpec((None, None, n_ctx, HEAD_DIM), lambda b, g, i: (b, g, 0, 0)),
                pl.BlockSpec((None, None, VT_ROWS, n_ctx), lambda b, g, i: (b, g, 0, 0))]
    args = [p_q, p_q, cos, sin, q_g, k_ctx, vt_ctx]
    scratch = [pltpu.VMEM((HEAD_DIM, GQA_REP * tq), BF16), pltpu.VMEM((VT_ROWS, GQA_REP * tq), F32)]
    if n_lat_chunks:
        n_lat = k_lat.shape[2]
        in_specs += [pl.BlockSpec((None, None, n_lat, HEAD_DIM), lambda b, g, i: (b, g, 0, 0)),
                     pl.BlockSpec((None, None, VT_ROWS, n_lat), lambda b, g, i: (b, g, 0, 0))]
        args += [k_lat, vt_lat]
        scratch += [pltpu.VMEM((tk, GQA_REP * tq), F32), pltpu.VMEM((tk, GQA_REP * tq), F32)]
    return pl.pallas_call(
        functools.partial(_attn_kernel, tq=tq, tk=tk, n_lat_chunks=n_lat_chunks),
        out_shape=jax.ShapeDtypeStruct((T, D_MODEL), BF16),
        grid=(batch, N_KV_HEADS, nq),
        in_specs=in_specs,
        out_specs=pl.BlockSpec((tq, gw), lambda b, g, i: (b * nq + i, g)),
        scratch_shapes=scratch,
        compiler_params=_cparams(("arbitrary", "arbitrary", "arbitrary")),
        name="gqa_flash_lat" if n_lat_chunks else "gqa_flash_ctx",
    )(*args)


def _merge_kernel(u_ref, v_ref, ga_ref, sw_ref, sbias_ref, sgain_ref, sf_ref, sb_ref, z_ref, yc_ref, gm_ref,
                  x_ref, mod_ref, wb_ref, wo_ref, bo_ref, sg_ref, lg_ref, lb_ref, out_ref, ya_ref):
    _sgu_kernel(u_ref, v_ref, ga_ref, sw_ref, sbias_ref, sgain_ref, ya_ref, n_chunks=u_ref.shape[0] // CHUNK)
    s = (sf_ref[...].astype(F32) + sb_ref[...].astype(F32)) * _silu(z_ref[...].astype(F32))
    yb = s * lax.rsqrt(jnp.mean(s * s, axis=-1, keepdims=True) + RMS_EPS) * sg_ref[...]
    gate_of = lambda k: jax.nn.sigmoid(gm_ref[:, k * D_MODEL:(k + 1) * D_MODEL].astype(F32))
    merged = gate_of(0) * jnp.dot(ya_ref[...], wb_ref[0], preferred_element_type=F32)
    merged = merged + gate_of(1) * jnp.dot(yb.astype(BF16), wb_ref[1], preferred_element_type=F32)
    merged = merged + gate_of(2) * jnp.dot(yc_ref[...], wb_ref[2], preferred_element_type=F32)
    out = jnp.dot(merged.astype(BF16), wo_ref[...], preferred_element_type=F32) + bo_ref[...]
    gate = mod_ref[:, 2 * D_MODEL:3 * D_MODEL]
    r = DEEPNORM_ALPHA * x_ref[...] + gate * out
    mu = jnp.mean(r, axis=-1, keepdims=True)
    rc = r - mu
    var = jnp.mean(rc * rc, axis=-1, keepdims=True)
    out_ref[...] = rc * lax.rsqrt(var + LN_EPS) * lg_ref[...] + lb_ref[...]


def _merge(sgu_w, sgu_bfull, sgu_g, s_f, s_b, p, y_c, x2d, mod, w_branch, w_out, b_out, ssm_g, ln_g, ln_b,
           tm, mod_row):
    T = x2d.shape[0]
    row = lambda: pl.BlockSpec((tm, D_MODEL), lambda i: (i, 0))
    pcol = lambda off: pl.BlockSpec((tm, D_MODEL), lambda i: (i, off // D_MODEL))
    vec = lambda: pl.BlockSpec((1, D_MODEL), lambda i: (0, 0))
    once = pl.Buffered(1)
    return pl.pallas_call(
        _merge_kernel,
        out_shape=jax.ShapeDtypeStruct((T, D_MODEL), F32),
        grid=(T // tm,),
        in_specs=[pcol(OFF_U), pcol(OFF_V), pcol(OFF_GA),
                  pl.BlockSpec((A_GROUPS, CHUNK, CHUNK), lambda i: (0, 0, 0)),
                  pl.BlockSpec((CHUNK, D_MODEL), lambda i: (0, 0)),
                  vec(),
                  row(), row(), pcol(OFF_Z), row(),
                  pl.BlockSpec((tm, 3 * D_MODEL), lambda i: (i, OFF_GM // (3 * D_MODEL))),
                  row(),
                  pl.BlockSpec((None, 1, 3 * D_MODEL), lambda i: (mod_row(i), 0, 0)),
                  pl.BlockSpec((3, D_MODEL, D_MODEL), lambda i: (0, 0, 0), pipeline_mode=once),
                  pl.BlockSpec((D_MODEL, D_MODEL), lambda i: (0, 0), pipeline_mode=once),
                  vec(), vec(), vec(), vec()],
        out_specs=row(),
        scratch_shapes=[pltpu.VMEM((tm, D_MODEL), BF16)],
        compiler_params=_cparams(("arbitrary",)),
        name="sgu_merge_out_deepnorm",
    )(p, p, p, sgu_w, sgu_bfull, sgu_g, s_f, s_b, p, y_c, p, x2d, mod, w_branch, w_out, b_out, ssm_g, ln_g, ln_b)


def _pack_segments():
    ref_off, o = {}, 0
    for name, n in (("u", 1024), ("v", 1024), ("ga", 1024), ("z", 1024), ("xbc", CONV_DIM),
                    ("dt", 2 * SSM_HEADS), ("q", 1024), ("k", KV_WIDTH), ("vv", KV_WIDTH),
                    ("gc", 1024), ("gm", 3 * D_MODEL)):
        ref_off[name] = (o, n)
        o += n
    packed = (("u", OFF_U), ("v", OFF_V), ("ga", OFF_GA), ("z", OFF_Z), ("q", OFF_Q), ("gc", OFF_GC),
              ("gm", OFF_GM), ("xbc", OFF_XBC), ("k", OFF_K), ("vv", OFF_VV), ("dt", OFF_DT))
    return tuple((dst,) + ref_off[name] for name, dst in packed), o


PACK_SEGMENTS, N_IN = _pack_segments()


def _pack_kernel(w_ref, o_ref):
    for dst, src, n in PACK_SEGMENTS:
        o_ref[dst:dst + n, :] = w_ref[src:src + n, :].astype(BF16)
    used = OFF_DT + 2 * SSM_HEADS
    o_ref[used:, :] = jnp.zeros((N_PACK - used, o_ref.shape[1]), BF16)


def _pack_w_in(w_in_t, tl=256):
    L, n_in, d = w_in_t.shape
    assert n_in == N_IN
    return pl.pallas_call(
        _pack_kernel,
        out_shape=jax.ShapeDtypeStruct((L, N_PACK, d), BF16),
        grid=(L, d // tl),
        in_specs=[pl.BlockSpec((None, n_in, tl), lambda l, i: (l, 0, i))],
        out_specs=pl.BlockSpec((None, N_PACK, tl), lambda l, i: (l, 0, i)),
        compiler_params=_cparams(("arbitrary", "arbitrary")),
        name="pack_w_in",
    )(w_in_t)


def _rope_tables(n_tokens):
    rows = n_tokens // GRID_W
    row = np.repeat(np.arange(rows, dtype=np.float32), GRID_W)
    col = np.tile(np.arange(GRID_W, dtype=np.float32), rows)
    inv = np.float32(ROPE_THETA) ** (-np.arange(ROPE_PAIRS_PER_AXIS, dtype=np.float32) / ROPE_PAIRS_PER_AXIS)
    ang = np.concatenate([row[:, None] * inv, col[:, None] * inv], axis=-1).astype(np.float32)
    cos, sin = np.cos(ang), np.sin(ang)
    cos_f = np.stack([cos, cos], axis=-1).reshape(n_tokens, HEAD_DIM)
    sin_f = np.stack([-sin, sin], axis=-1).reshape(n_tokens, HEAD_DIM)
    return jnp.asarray(cos_f, F32), jnp.asarray(sin_f, F32)


def _pad_lanes(v, n):
    return jnp.concatenate([v, jnp.zeros((n - v.shape[0],), v.dtype)])[None, :]


def kernel(x, c, ctx, c_ctx, w_mod, b_mod, w_in, sgu_w, sgu_b, sgu_norm_g, conv_w, conv_b, dt_bias,
           a_log, d_skip, ssm_norm_g, q_norm_g, k_norm_g, w_branch, w_out, b_out, ln_g, ln_b):
    batch, seq, _ = x.shape
    n_ctx = ctx.shape[1]
    assert n_ctx == N_CTX_CHUNKS * CHUNK and seq % 1024 == 0

    cos_l, sin_l = _rope_tables(seq)
    cos_c = jnp.ones((n_ctx, HEAD_DIM), F32)
    sin_c = jnp.zeros((n_ctx, HEAD_DIM), F32)

    c_rows = jnp.concatenate([c, c_ctx[None, :], jnp.zeros((8 - batch - 1, D_MODEL), F32)], axis=0)
    mod_all = _modulation(c_rows, w_mod, b_mod[:, None, :])
    ctx_row = batch

    hid = jnp.arange(CHUNK)[:, None]
    ch = jnp.arange(D_MODEL)[None, :] // SSM_HEAD_DIM
    e_mats = jnp.stack([(hid == ch + d * SSM_HEADS).astype(BF16) for d in range(2)])

    lat_per_tile = 1024
    tiles_per_batch = seq // lat_per_tile
    xl = x.reshape(batch * seq, D_MODEL)
    xc = ctx.reshape(batch * n_ctx, D_MODEL)

    w_pack_all = _pack_w_in(jnp.swapaxes(w_in, 1, 2))
    shift_mat = _conv_shift_matrix()

    for l in range(DEPTH):
        ctx_out = l < DEPTH - 1
        mod = mod_all[l][:, None, :]
        wb = w_branch[l].astype(BF16)
        wo = w_out[l].astype(BF16)
        q_g, k_g = q_norm_g[l][None, :], k_norm_g[l][None, :]
        sgu_wb = sgu_w[l].astype(BF16)
        sgu_bfull = jnp.repeat(sgu_b[l].T, A_GROUP_DIM, axis=1)
        sgu_g = sgu_norm_g[l][None, :]
        dtb = _pad_lanes(dt_bias[l].reshape(-1), CHUNK)
        alog = _pad_lanes(a_log[l].reshape(-1), CHUNK)

        lat_row = lambda i: i // tiles_per_batch
        p_lat, dt_lat = _inproj(xl, mod, w_pack_all, l, lat_per_tile, lat_row)
        p_ctx, dt_ctx = _inproj(xc, mod, w_pack_all, l, n_ctx, lambda i: ctx_row)

        xconv_lat = _conv(p_lat, shift_mat, conv_w[l], conv_b[l][None, :], seq, 512)
        xconv_ctx = _conv(p_ctx, shift_mat, conv_w[l], conv_b[l][None, :], n_ctx, n_ctx)
        dskip_x = jnp.repeat(d_skip[l], SSM_HEAD_DIM, axis=1)[:, None, :]
        ssd = _ssd(xconv_ctx, xconv_lat, dt_ctx, dt_lat, dtb, alog, dskip_x, e_mats, batch)

        k_ctx, vt_ctx = _kvprep(p_ctx, cos_c, sin_c, k_g, batch, n_ctx)
        k_lat, vt_lat = _kvprep(p_lat, cos_l, sin_l, k_g, batch, 512)
        o_lat = _attention(p_lat, cos_l, sin_l, q_g, k_ctx, vt_ctx, k_lat, vt_lat, batch, 512, 1024)

        sgu = (sgu_wb, sgu_bfull, sgu_g)
        vecs = (b_out[l][None, :], ssm_norm_g[l][None, :], ln_g[l][None, :], ln_b[l][None, :])
        merge_tm = 512
        merge_tiles_per_batch = seq // merge_tm
        xl_new = _merge(*sgu, ssd[0][1], ssd[1][1], p_lat, o_lat, xl, mod, wb, wo, *vecs,
                        merge_tm, lambda i: i // merge_tiles_per_batch)
        if ctx_out:
            o_ctx = _attention(p_ctx, cos_c, sin_c, q_g, k_ctx, vt_ctx, None, None, batch, n_ctx, 0)
            xc = _merge(*sgu, ssd[0][0], ssd[1][0], p_ctx, o_ctx, xc, mod, wb, wo, *vecs,
                        n_ctx, lambda i: ctx_row)
        xl = xl_new

    return xl.reshape(batch, seq, D_MODEL)
```

```python
import functools
import math

import jax
import jax.numpy as jnp
import numpy as np
from jax import lax
from jax.experimental import pallas as pl
from jax.experimental.pallas import tpu as pltpu

D_MODEL = 1024
DEPTH = 2
GRID_W = 64
CHUNK = 128
A_GROUPS = 4
A_GROUP_DIM = D_MODEL // A_GROUPS
SSM_HEAD_DIM = 64
SSM_HEADS = 16
SSM_GROUPS = 2
SSM_STATE = 128
D_CONV = 5
CONV_DIM = D_MODEL + 2 * SSM_GROUPS * SSM_STATE
HEAD_DIM = 128
N_Q_HEADS = 8
N_KV_HEADS = 2
GQA_REP = N_Q_HEADS // N_KV_HEADS
KV_WIDTH = N_KV_HEADS * HEAD_DIM
ROPE_THETA = 10000.0
ROPE_PAIRS_PER_AXIS = HEAD_DIM // 4
DEEPNORM_ALPHA = (2 * DEPTH) ** 0.25
LN_EPS = 1e-6
RMS_EPS = 1e-6

OFF_U, OFF_V, OFF_GA, OFF_Z, OFF_Q, OFF_GC, OFF_GM = 0, 1024, 2048, 3072, 4096, 5120, 6144
OFF_XBC = 9216
OFF_K = OFF_XBC + CONV_DIM
OFF_VV = OFF_K + KV_WIDTH
OFF_DT = OFF_VV + KV_WIDTH
DT_PAD = 256
N_PACK = OFF_DT + DT_PAD
PROJ_TN = 1280

BF16_SUBLANES = 16
VT_ROWS = HEAD_DIM + BF16_SUBLANES

VMEM_LIMIT = 56 * 1024 * 1024
F32 = jnp.float32
BF16 = jnp.bfloat16
NEG_BIG = -1e30


def _cparams(sem):
    return pltpu.CompilerParams(dimension_semantics=sem, vmem_limit_bytes=VMEM_LIMIT)


def _silu(t):
    return t * jax.nn.sigmoid(t)


def _split_dot(x, e, terms=3):
    acc = None
    r = x
    for t in range(terms):
        xt = r.astype(BF16)
        if t + 1 < terms:
            r = r - xt.astype(F32)
        part = jnp.dot(xt, e, preferred_element_type=F32)
        acc = part if acc is None else acc + part
    return acc


def _split_dot_rhs(e, x):
    x1 = x.astype(BF16)
    r1 = x - x1.astype(F32)
    x2 = r1.astype(BF16)
    r2 = r1 - x2.astype(F32)
    x3 = r2.astype(BF16)
    acc = jnp.dot(e, x1, preferred_element_type=F32)
    acc = acc + jnp.dot(e, x2, preferred_element_type=F32)
    return acc + jnp.dot(e, x3, preferred_element_type=F32)


def _mod_kernel(c_ref, w_ref, b_ref, o_ref):
    c = c_ref[...]
    act = _silu(c)
    o_ref[...] = jnp.dot(act, w_ref[...], preferred_element_type=F32,
                         precision=lax.Precision.HIGHEST) + b_ref[...]


def _modulation(c_rows, w_mod, b_mod):
    L = w_mod.shape[0]
    tn = 512
    return pl.pallas_call(
        _mod_kernel,
        out_shape=jax.ShapeDtypeStruct((L, 8, 3 * D_MODEL), F32),
        grid=(L, 3 * D_MODEL // tn),
        in_specs=[pl.BlockSpec((8, D_MODEL), lambda l, j: (0, 0)),
                  pl.BlockSpec((None, D_MODEL, tn), lambda l, j: (l, 0, j)),
                  pl.BlockSpec((None, 1, tn), lambda l, j: (l, 0, j))],
        out_specs=pl.BlockSpec((None, 8, tn), lambda l, j: (l, 0, j)),
        compiler_params=_cparams(("arbitrary", "arbitrary")),
        name="adaln_modulation",
    )(c_rows, w_mod, b_mod)


def _inproj_kernel(x_ref, mod_ref, w_ref, o_ref, dt_ref, h_ref):
    j = pl.program_id(1)

    @pl.when(j == 0)
    def _():
        x = x_ref[...]
        mu = jnp.mean(x, axis=-1, keepdims=True)
        xc = x - mu
        var = jnp.mean(xc * xc, axis=-1, keepdims=True)
        shift = mod_ref[:, 0:D_MODEL]
        scale = mod_ref[:, D_MODEL:2 * D_MODEL]
        h = xc * lax.rsqrt(var + LN_EPS) * (1.0 + scale) + shift
        h_ref[...] = h.astype(BF16)

    acc = lax.dot_general(h_ref[...], w_ref[...], (((1,), (1,)), ((), ())), preferred_element_type=F32)
    o_ref[...] = acc.astype(o_ref.dtype)

    @pl.when(j == pl.num_programs(1) - 1)
    def _():
        dt_ref[...] = acc[:, PROJ_TN - DT_PAD:]


def _inproj(x2d, mod, w_pack, layer, tm, mod_row):
    T = x2d.shape[0]
    assert OFF_DT + DT_PAD == N_PACK and N_PACK % PROJ_TN == 0
    return pl.pallas_call(
        _inproj_kernel,
        out_shape=(jax.ShapeDtypeStruct((T, N_PACK), BF16),
                   jax.ShapeDtypeStruct((T, DT_PAD), F32)),
        grid=(T // tm, N_PACK // PROJ_TN),
        in_specs=[pl.BlockSpec((tm, D_MODEL), lambda i, j: (i, 0)),
                  pl.BlockSpec((None, 1, 3 * D_MODEL), lambda i, j: (mod_row(i), 0, 0)),
                  pl.BlockSpec((None, PROJ_TN, D_MODEL), lambda i, j: (layer, j, 0))],
        out_specs=(pl.BlockSpec((tm, PROJ_TN), lambda i, j: (i, j)),
                   pl.BlockSpec((tm, DT_PAD), lambda i, j: (i, 0))),
        scratch_shapes=[pltpu.VMEM((tm, D_MODEL), BF16)],
        compiler_params=_cparams(("arbitrary", "arbitrary")),
        name="ln_mod_inproj",
    )(x2d, mod, w_pack)


def _sgu_kernel(u_ref, v_ref, ga_ref, w_ref, b_ref, g_ref, o_ref, *, n_chunks):
    for c in range(n_chunks):
        rows = slice(c * CHUNK, (c + 1) * CHUNK)
        for g in range(A_GROUPS):
            cols = slice(g * A_GROUP_DIM, (g + 1) * A_GROUP_DIM)
            v = v_ref[rows, cols].astype(F32)
            mu = jnp.mean(v, axis=-1, keepdims=True)
            vc = v - mu
            var = jnp.mean(vc * vc, axis=-1, keepdims=True)
            vn = vc * lax.rsqrt(var + LN_EPS) * g_ref[:, cols]
            mixed = jnp.dot(w_ref[g], vn.astype(BF16), preferred_element_type=F32)
            mixed = mixed + b_ref[:, cols]
            y = u_ref[rows, cols].astype(F32) * mixed * _silu(ga_ref[rows, cols].astype(F32))
            o_ref[rows, cols] = y.astype(o_ref.dtype)


CONV_HALO = 64
CONV_WINDOW = CHUNK + 2 * CONV_HALO


def _conv_shift_matrix():
    pad = (D_CONV - 1) // 2
    m = np.zeros((D_CONV * CHUNK, CONV_WINDOW), np.float32)
    for j in range(D_CONV):
        t = np.arange(CHUNK)
        m[j * CHUNK + t, t + CONV_HALO + j - pad] = 1.0
    return jnp.asarray(m, BF16)


def _conv_kernel(prev_ref, cur_ref, next_ref, shift_ref, w_ref, b_ref, o_ref, ext_ref, *, tiles_per_seq, tc):
    i = pl.program_id(0)
    pos = i % tiles_per_seq
    zeros = jnp.zeros((CONV_HALO, CONV_DIM), BF16)
    ext_ref[0:CONV_HALO, :] = jnp.where(pos == 0, zeros, prev_ref[...])
    ext_ref[CONV_HALO:CONV_HALO + tc, :] = cur_ref[...]
    ext_ref[CONV_HALO + tc:, :] = jnp.where(pos == tiles_per_seq - 1, zeros, next_ref[...])
    for k in range(tc // CHUNK):
        window = ext_ref[k * CHUNK:k * CHUNK + CONV_WINDOW, :]
        taps = jnp.dot(shift_ref[...], window, preferred_element_type=F32)
        acc = b_ref[...] + taps[0:CHUNK, :] * w_ref[0:1, :]
        for j in range(1, D_CONV):
            acc = acc + taps[j * CHUNK:(j + 1) * CHUNK, :] * w_ref[j:j + 1, :]
        o_ref[k * CHUNK:(k + 1) * CHUNK, :] = _silu(acc).astype(o_ref.dtype)


def _conv(p, shift_mat, conv_w, conv_b, seq_len, tc):
    T = p.shape[0]
    n_tiles = T // tc
    hb = tc // CONV_HALO
    n_hblocks = T // CONV_HALO
    colb = OFF_XBC // CONV_DIM
    return pl.pallas_call(
        functools.partial(_conv_kernel, tiles_per_seq=seq_len // tc, tc=tc),
        out_shape=jax.ShapeDtypeStruct((T, CONV_DIM), BF16),
        grid=(n_tiles,),
        in_specs=[pl.BlockSpec((CONV_HALO, CONV_DIM), lambda i: (jnp.maximum(i * hb - 1, 0), colb)),
                  pl.BlockSpec((tc, CONV_DIM), lambda i: (i, colb)),
                  pl.BlockSpec((CONV_HALO, CONV_DIM),
                               lambda i: (jnp.minimum((i + 1) * hb, n_hblocks - 1), colb)),
                  pl.BlockSpec((D_CONV * CHUNK, CONV_WINDOW), lambda i: (0, 0)),
                  pl.BlockSpec((D_CONV, CONV_DIM), lambda i: (0, 0)),
                  pl.BlockSpec((1, CONV_DIM), lambda i: (0, 0))],
        out_specs=pl.BlockSpec((tc, CONV_DIM), lambda i: (i, 0)),
        scratch_shapes=[pltpu.VMEM((tc + 2 * CONV_HALO, CONV_DIM), BF16)],
        compiler_params=_cparams(("arbitrary",)),
        name="dwconv_silu",
    )(p, p, p, shift_mat, conv_w, conv_b)


N_CTX_CHUNKS = 2


SSD_BLOCK = 512


def _ssd_scan(dt_raw, dtb, alog, d):
    reverse = d == 1
    dt = jax.nn.softplus(dt_raw + dtb)
    a = dt * (-jnp.exp(alog) * LOG2E)

    row = lax.broadcasted_iota(jnp.int32, (CHUNK, CHUNK), 0)
    col = lax.broadcasted_iota(jnp.int32, (CHUNK, CHUNK), 1)
    mask = (row <= col) if reverse else (row >= col)
    tri = mask.astype(BF16)
    tri_t = ((row >= col) if reverse else (row <= col)).astype(BF16)

    a_cs = _split_dot_rhs(tri, a)
    a_cs_t = _split_dot(a.T, tri_t)
    return dt, a_cs, a_cs_t, mask


def _ssd_group(g, scan, x_cols, dskip_ref, e_ref, ht_ref, y_ref, d):
    dt, a_cs, a_cs_t, mask = scan
    end = 0 if d == 1 else CHUNK - 1
    lane = lax.broadcasted_iota(jnp.int32, (CHUNK, 2 * SSM_HEAD_DIM), 1)
    left = lane < SSM_HEAD_DIM
    gw = D_MODEL // SSM_GROUPS
    heads_per_group = SSM_HEADS // SSM_GROUPS
    heads_per_block = SSD_BLOCK // SSM_HEAD_DIM
    b_g = x_cols(D_MODEL + g * SSM_STATE, D_MODEL + (g + 1) * SSM_STATE)
    c_g = x_cols(D_MODEL + (SSM_GROUPS + g) * SSM_STATE, D_MODEL + (SSM_GROUPS + g + 1) * SSM_STATE)
    cb = lax.dot_general(c_g, b_g, (((1,), (1,)), ((), ())), preferred_element_type=F32)
    b_t = b_g.astype(F32).T.astype(BF16)
    for blk in range(gw // SSD_BLOCK):
        c0 = g * gw + blk * SSD_BLOCK
        cols = slice(c0, c0 + SSD_BLOCK)
        gcols = slice(blk * SSD_BLOCK, (blk + 1) * SSD_BLOCK)
        e_blk = e_ref[:, cols]
        a_cs_x = _split_dot(a_cs, e_blk, terms=2)
        dt_x = _split_dot(dt, e_blk, terms=1)
        a_tot_x = a_cs_x[end:end + 1, :]
        xs = x_cols(c0, c0 + SSD_BLOCK).astype(F32)
        xdt = xs * dt_x
        xdt_bf = xdt.astype(BF16)
        xw = (xdt * jnp.exp2(a_tot_x - a_cs_x)).astype(BF16)
        h_in = ht_ref[g, :, gcols]
        y_blk = jnp.dot(c_g, h_in.astype(BF16), preferred_element_type=F32) * jnp.exp2(a_cs_x)
        ht_ref[g, :, gcols] = h_in * jnp.exp2(a_tot_x) + jnp.dot(b_t, xw, preferred_element_type=F32)
        y_pairs = []
        for j in range(heads_per_block // 2):
            ms = []
            for k in range(2):
                hh = d * SSM_HEADS + g * heads_per_group + blk * heads_per_block + 2 * j + k
                seg = a_cs[:, hh:hh + 1] - a_cs_t[hh:hh + 1, :]
                dec = jnp.exp2(jnp.where(mask, seg, NEG_BIG))
                ms.append((cb * dec).astype(BF16))
            lhs = jnp.concatenate(ms, axis=1)
            xp = xdt_bf[:, 2 * j * SSM_HEAD_DIM:(2 * j + 2) * SSM_HEAD_DIM]
            zero = jnp.zeros_like(xp)
            rhs = jnp.concatenate([jnp.where(left, xp, zero), jnp.where(left, zero, xp)], axis=0)
            y_pairs.append(jnp.dot(lhs, rhs, preferred_element_type=F32))
        y_blk = y_blk + jnp.concatenate(y_pairs, axis=1) + dskip_ref[:, cols] * xs
        y_ref[:, cols] = y_blk.astype(y_ref.dtype)


def _ssd_kernel(xcf_ref, xlf_ref, dtcf_ref, dtlf_ref, xcr_ref, xlr_ref, dtcr_ref, dtlr_ref,
                dtb_ref, alog_ref, dskip_ref, e_ref,
                ycf_ref, ylf_ref, ycr_ref, ylr_ref, ht_ref, y_ref):
    s = pl.program_id(0)
    is_ctx = s < N_CTX_CHUNKS
    batch = xcf_ref.shape[0]

    @pl.when(s == 0)
    def _():
        ht_ref[...] = jnp.zeros_like(ht_ref)

    sides = ((xcf_ref, xlf_ref, dtcf_ref, dtlf_ref, ycf_ref, ylf_ref),
             (xcr_ref, xlr_ref, dtcr_ref, dtlr_ref, ycr_ref, ylr_ref))
    chains = []
    for b in range(batch):
        for d, (xc_ref, xl_ref, dtc_ref, dtl_ref, _, _) in enumerate(sides):
            x_cols = lambda c0, c1, b=b, xc_ref=xc_ref, xl_ref=xl_ref: jnp.where(
                is_ctx, xc_ref[b, :, c0:c1], xl_ref[b, :, c0:c1])
            dt_raw = jnp.where(is_ctx, dtc_ref[b, :, 0:CHUNK], dtl_ref[b, :, 0:CHUNK])
            chains.append((b, d, x_cols, _ssd_scan(dt_raw, dtb_ref[...], alog_ref[...], d)))
    for g in range(SSM_GROUPS):
        for b, d, x_cols, scan in chains:
            _ssd_group(g, scan, x_cols, dskip_ref.at[d], e_ref.at[d], ht_ref.at[b, d], y_ref.at[b, d], d)

    @pl.when(is_ctx)
    def _():
        for d in range(2):
            sides[d][4][...] = y_ref[:, d]

    @pl.when(jnp.logical_not(is_ctx))
    def _():
        for d in range(2):
            sides[d][5][...] = y_ref[:, d]


def _ssd(xconv_ctx, xconv_lat, dt_ctx, dt_lat, dtb, alog, dskip_x, e_mats, batch):
    n_lat = xconv_lat.shape[0] // batch // CHUNK
    n_steps = N_CTX_CHUNKS + n_lat
    per_batch = lambda t: t.reshape(batch, t.shape[0] // batch, t.shape[1])

    cidx = (lambda s: jnp.minimum(s, N_CTX_CHUNKS - 1),
            lambda s: jnp.maximum(N_CTX_CHUNKS - 1 - s, 0))
    lidx = (lambda s: jnp.maximum(s - N_CTX_CHUNKS, 0),
            lambda s: jnp.minimum(n_steps - 1 - s, n_lat - 1))

    def specs(width, d):
        return [pl.BlockSpec((batch, CHUNK, width), lambda s, d=d: (0, cidx[d](s), 0)),
                pl.BlockSpec((batch, CHUNK, width), lambda s, d=d: (0, lidx[d](s), 0))]

    const = lambda shape: pl.BlockSpec(shape, lambda s: (0,) * len(shape))
    in_specs, out_specs, args = [], [], []
    for d in range(2):
        in_specs += specs(CONV_DIM, d) + specs(DT_PAD, d)
        args += [per_batch(xconv_ctx), per_batch(xconv_lat), per_batch(dt_ctx), per_batch(dt_lat)]
        out_specs += specs(D_MODEL, d)
    in_specs += [const((1, CHUNK)), const((1, CHUNK)), const((2, 1, D_MODEL)), const((2, CHUNK, D_MODEL))]
    y_shapes = (jax.ShapeDtypeStruct((batch, xconv_ctx.shape[0] // batch, D_MODEL), BF16),
                jax.ShapeDtypeStruct((batch, xconv_lat.shape[0] // batch, D_MODEL), BF16))
    ys = pl.pallas_call(
        _ssd_kernel,
        out_shape=y_shapes + y_shapes,
        grid=(n_steps,),
        in_specs=in_specs,
        out_specs=tuple(out_specs),
        scratch_shapes=[pltpu.VMEM((batch, 2, SSM_GROUPS, SSM_STATE, D_MODEL // SSM_GROUPS), F32),
                        pltpu.VMEM((batch, 2, CHUNK, D_MODEL), BF16)],
        compiler_params=_cparams(("arbitrary",)),
        name="ssd_scan_bidir",
    )(*args, dtb, alog, dskip_x, e_mats)
    ycf, ylf, ycr, ylr = (y.reshape(-1, D_MODEL) for y in ys)
    return (ycf, ylf), (ycr, ylr)


def _norm_rope(t, g, cos, sin):
    tn = t * lax.rsqrt(jnp.mean(t * t, axis=-1, keepdims=True) + RMS_EPS) * g
    lane = lax.broadcasted_iota(jnp.int32, tn.shape, 1)
    partner = jnp.where(lane % 2 == 0, pltpu.roll(tn, HEAD_DIM - 1, axis=1), pltpu.roll(tn, 1, axis=1))
    return tn * cos + partner * sin


def _kvprep_kernel(k_ref, v_ref, cos_ref, sin_ref, g_ref, ko_ref, vto_ref):
    for h in range(N_KV_HEADS):
        cols = slice(h * HEAD_DIM, (h + 1) * HEAD_DIM)
        kh = _norm_rope(k_ref[:, cols].astype(F32), g_ref[...], cos_ref[...], sin_ref[...])
        ko_ref[h] = kh.astype(BF16)
        vto_ref[h, 0:HEAD_DIM, :] = v_ref[:, cols].astype(F32).T.astype(BF16)
        vto_ref[h, HEAD_DIM:, :] = jnp.ones((VT_ROWS - HEAD_DIM, k_ref.shape[0]), BF16)


def _kvprep(p, cos, sin, k_g, batch, tm):
    T = p.shape[0]
    S = T // batch
    nt = S // tm
    return pl.pallas_call(
        _kvprep_kernel,
        out_shape=(jax.ShapeDtypeStruct((batch, N_KV_HEADS, S, HEAD_DIM), BF16),
                   jax.ShapeDtypeStruct((batch, N_KV_HEADS, VT_ROWS, S), BF16)),
        grid=(batch, nt),
        in_specs=[pl.BlockSpec((tm, KV_WIDTH), lambda b, i: (b * nt + i, OFF_K // KV_WIDTH)),
                  pl.BlockSpec((tm, KV_WIDTH), lambda b, i: (b * nt + i, OFF_VV // KV_WIDTH)),
                  pl.BlockSpec((tm, HEAD_DIM), lambda b, i: (i, 0)),
                  pl.BlockSpec((tm, HEAD_DIM), lambda b, i: (i, 0)),
                  pl.BlockSpec((1, HEAD_DIM), lambda b, i: (0, 0))],
        out_specs=(pl.BlockSpec((None, N_KV_HEADS, tm, HEAD_DIM), lambda b, i: (b, 0, i, 0)),
                   pl.BlockSpec((None, N_KV_HEADS, VT_ROWS, tm), lambda b, i: (b, 0, 0, i))),
        compiler_params=_cparams(("arbitrary", "arbitrary")),
        name="kv_prep",
    )(p, p, cos, sin, k_g)


LOG2E = 1.4426950408889634
ATTN_LANE_BLOCK = 256


def _attn_kernel(*refs, tq, tk, n_lat_chunks):
    if n_lat_chunks:
        (q_ref, gc_ref, cos_ref, sin_ref, g_ref, kc_ref, vtc_ref, kl_ref, vtl_ref, o_ref,
         qt_ref, acc_ref, s0_ref, s1_ref) = refs
    else:
        q_ref, gc_ref, cos_ref, sin_ref, g_ref, kc_ref, vtc_ref, o_ref, qt_ref, acc_ref = refs
    scale = HEAD_DIM ** -0.5 * LOG2E

    lb = min(tq, ATTN_LANE_BLOCK)
    blocks = [(h, r0, slice(h * tq + r0, h * tq + r0 + lb)) for h in range(GQA_REP) for r0 in range(0, tq, lb)]

    def offset(j):
        return j * tk if isinstance(j, int) else pl.multiple_of(j * tk, tk)

    def k_chunk(j):
        return kl_ref[pl.ds(offset(j), tk), :]

    def vt_chunk(j):
        return vtl_ref[:, pl.ds(offset(j), tk)]

    def finish(acc, h, r0):
        cols = slice(h * HEAD_DIM, (h + 1) * HEAD_DIM)
        rows = slice(r0, r0 + lb)
        oh = (acc[0:HEAD_DIM, :] * (1.0 / acc[HEAD_DIM:HEAD_DIM + 1, :])).T
        o_ref[rows, cols] = (oh * _silu(gc_ref[rows, cols].astype(F32))).astype(o_ref.dtype)

    m, cm0 = [], []
    for h, r0, lanes in blocks:
        cols = slice(h * HEAD_DIM, (h + 1) * HEAD_DIM)
        rows = slice(r0, r0 + lb)
        qh = _norm_rope(q_ref[rows, cols].astype(F32), g_ref[...], cos_ref[rows, :], sin_ref[rows, :]) * scale
        qt_ref[:, lanes] = qh.T.astype(BF16)
        st = jnp.dot(kc_ref[...], qt_ref[:, lanes], preferred_element_type=F32)
        mb = jnp.max(st, axis=0, keepdims=True)
        p = jnp.exp2(st - mb)
        acc = jnp.dot(vtc_ref[...], p.astype(BF16), preferred_element_type=F32)
        if n_lat_chunks:
            acc_ref[:, lanes] = acc
        else:
            finish(acc, h, r0)
        m.append(mb)
        if n_lat_chunks:
            st = jnp.dot(k_chunk(0), qt_ref[:, lanes], preferred_element_type=F32)
            s0_ref[:, lanes] = st
            cm0.append(jnp.max(st, axis=0, keepdims=True))
    m, cm0 = tuple(m), tuple(cm0)

    if n_lat_chunks:
        def stage(s_cur, s_next, j_cur, j_next, m, cm_cur):
            m_out, cm_next = [], []
            for bi, (h, r0, lanes) in enumerate(blocks):
                if j_next is not None:
                    st = jnp.dot(k_chunk(j_next), qt_ref[:, lanes], preferred_element_type=F32)
                    s_next[:, lanes] = st
                    cm_next.append(jnp.max(st, axis=0, keepdims=True))
                m_new = jnp.maximum(m[bi], cm_cur[bi])
                alpha = jnp.exp2(m[bi] - m_new)
                p = jnp.exp2(s_cur[:, lanes] - m_new)
                acc = alpha * acc_ref[:, lanes] + jnp.dot(vt_chunk(j_cur), p.astype(BF16),
                                                          preferred_element_type=F32)
                if j_next is not None:
                    acc_ref[:, lanes] = acc
                else:
                    finish(acc, h, r0)
                m_out.append(m_new)
            return tuple(m_out), tuple(cm_next)

        def pair(jj, carry):
            m, cm0 = carry
            j = 2 * jj
            m, cm1 = stage(s0_ref, s1_ref, j, j + 1, m, cm0)
            m, cm0 = stage(s1_ref, s0_ref, j + 1, j + 2, m, cm1)
            return m, cm0

        m, cm0 = lax.fori_loop(0, n_lat_chunks // 2 - 1, pair, (m, cm0))
        j = n_lat_chunks - 2
        m, cm1 = stage(s0_ref, s1_ref, j, j + 1, m, cm0)
        m, _ = stage(s1_ref, s0_ref, j + 1, None, m, cm1)


def _attention(p_q, cos, sin, q_g, k_ctx, vt_ctx, k_lat, vt_lat, batch, tq, tk):
    T = p_q.shape[0]
    S = T // batch
    nq = S // tq
    n_ctx = k_ctx.shape[2]
    n_lat_chunks = 0 if k_lat is None else k_lat.shape[2] // tk
    assert n_lat_chunks % 2 == 0
    gw = GQA_REP * HEAD_DIM
    in_specs = [pl.BlockSpec((tq, gw), lambda b, g, i: (b * nq + i, OFF_Q // gw + g)),
                pl.BlockSpec((tq, gw), lambda b, g, i: (b * nq + i, OFF_GC // gw + g)),
                pl.BlockSpec((tq, HEAD_DIM), lambda b, g, i: (i, 0)),
                pl.BlockSpec((tq, HEAD_DIM), lambda b, g, i: (i, 0)),
                pl.BlockSpec((1, HEAD_DIM), lambda b, g, i: (0, 0)),
                pl.BlockSpec((None, None, n_ctx, HEAD_DIM), lambda b, g, i: (b, g, 0, 0)),
                pl.BlockSpec((None, None, VT_ROWS, n_ctx), lambda b, g, i: (b, g, 0, 0))]
    args = [p_q, p_q, cos, sin, q_g, k_ctx, vt_ctx]
    scratch = [pltpu.VMEM((HEAD_DIM, GQA_REP * tq), BF16), pltpu.VMEM((VT_ROWS, GQA_REP * tq), F32)]
    if n_lat_chunks:
        n_lat = k_lat.shape[2]
        in_specs += [pl.BlockSpec((None, None, n_lat, HEAD_DIM), lambda b, g, i: (b, g, 0, 0)),
                     pl.BlockSpec((None, None, VT_ROWS, n_lat), lambda b, g, i: (b, g, 0, 0))]
        args += [k_lat, vt_lat]
        scratch += [pltpu.VMEM((tk, GQA_REP * tq), F32), pltpu.VMEM((tk, GQA_REP * tq), F32)]
    return pl.pallas_call(
        functools.partial(_attn_kernel, tq=tq, tk=tk, n_lat_chunks=n_lat_chunks),
        out_shape=jax.ShapeDtypeStruct((T, D_MODEL), BF16),
        grid=(batch, N_KV_HEADS, nq),
        in_specs=in_specs,
        out_specs=pl.BlockSpec((tq, gw), lambda b, g, i: (b * nq + i, g)),
        scratch_shapes=scratch,
        compiler_params=_cparams(("arbitrary", "arbitrary", "arbitrary")),
        name="gqa_flash_lat" if n_lat_chunks else "gqa_flash_ctx",
    )(*args)


def _merge_kernel(u_ref, v_ref, ga_ref, sw_ref, sbias_ref, sgain_ref, sf_ref, sb_ref, z_ref, yc_ref, gm_ref,
                  x_ref, mod_ref, wb_ref, wo_ref, bo_ref, sg_ref, lg_ref, lb_ref, out_ref, ya_ref):
    _sgu_kernel(u_ref, v_ref, ga_ref, sw_ref, sbias_ref, sgain_ref, ya_ref, n_chunks=u_ref.shape[0] // CHUNK)
    s = (sf_ref[...].astype(F32) + sb_ref[...].astype(F32)) * _silu(z_ref[...].astype(F32))
    yb = s * lax.rsqrt(jnp.mean(s * s, axis=-1, keepdims=True) + RMS_EPS) * sg_ref[...]
    gate_of = lambda k: jax.nn.sigmoid(gm_ref[:, k * D_MODEL:(k + 1) * D_MODEL].astype(F32))
    merged = gate_of(0) * jnp.dot(ya_ref[...], wb_ref[0], preferred_element_type=F32)
    merged = merged + gate_of(1) * jnp.dot(yb.astype(BF16), wb_ref[1], preferred_element_type=F32)
    merged = merged + gate_of(2) * jnp.dot(yc_ref[...], wb_ref[2], preferred_element_type=F32)
    out = jnp.dot(merged.astype(BF16), wo_ref[...], preferred_element_type=F32) + bo_ref[...]
    gate = mod_ref[:, 2 * D_MODEL:3 * D_MODEL]
    r = DEEPNORM_ALPHA * x_ref[...] + gate * out
    mu = jnp.mean(r, axis=-1, keepdims=True)
    rc = r - mu
    var = jnp.mean(rc * rc, axis=-1, keepdims=True)
    out_ref[...] = rc * lax.rsqrt(var + LN_EPS) * lg_ref[...] + lb_ref[...]


def _merge(sgu_w, sgu_bfull, sgu_g, s_f, s_b, p, y_c, x2d, mod, w_branch, w_out, b_out, ssm_g, ln_g, ln_b,
           tm, mod_row):
    T = x2d.shape[0]
    row = lambda: pl.BlockSpec((tm, D_MODEL), lambda i: (i, 0))
    pcol = lambda off: pl.BlockSpec((tm, D_MODEL), lambda i: (i, off // D_MODEL))
    vec = lambda: pl.BlockSpec((1, D_MODEL), lambda i: (0, 0))
    once = pl.Buffered(1)
    return pl.pallas_call(
        _merge_kernel,
        out_shape=jax.ShapeDtypeStruct((T, D_MODEL), F32),
        grid=(T // tm,),
        in_specs=[pcol(OFF_U), pcol(OFF_V), pcol(OFF_GA),
                  pl.BlockSpec((A_GROUPS, CHUNK, CHUNK), lambda i: (0, 0, 0)),
                  pl.BlockSpec((CHUNK, D_MODEL), lambda i: (0, 0)),
                  vec(),
                  row(), row(), pcol(OFF_Z), row(),
                  pl.BlockSpec((tm, 3 * D_MODEL), lambda i: (i, OFF_GM // (3 * D_MODEL))),
                  row(),
                  pl.BlockSpec((None, 1, 3 * D_MODEL), lambda i: (mod_row(i), 0, 0)),
                  pl.BlockSpec((3, D_MODEL, D_MODEL), lambda i: (0, 0, 0), pipeline_mode=once),
                  pl.BlockSpec((D_MODEL, D_MODEL), lambda i: (0, 0), pipeline_mode=once),
                  vec(), vec(), vec(), vec()],
        out_specs=row(),
        scratch_shapes=[pltpu.VMEM((tm, D_MODEL), BF16)],
        compiler_params=_cparams(("arbitrary",)),
        name="sgu_merge_out_deepnorm",
    )(p, p, p, sgu_w, sgu_bfull, sgu_g, s_f, s_b, p, y_c, p, x2d, mod, w_branch, w_out, b_out, ssm_g, ln_g, ln_b)


def _pack_segments():
    ref_off, o = {}, 0
    for name, n in (("u", 1024), ("v", 1024), ("ga", 1024), ("z", 1024), ("xbc", CONV_DIM),
                    ("dt", 2 * SSM_HEADS), ("q", 1024), ("k", KV_WIDTH), ("vv", KV_WIDTH),
                    ("gc", 1024), ("gm", 3 * D_MODEL)):
        ref_off[name] = (o, n)
        o += n
    packed = (("u", OFF_U), ("v", OFF_V), ("ga", OFF_GA), ("z", OFF_Z), ("q", OFF_Q), ("gc", OFF_GC),
              ("gm", OFF_GM), ("xbc", OFF_XBC), ("k", OFF_K), ("vv", OFF_VV), ("dt", OFF_DT))
    return tuple((dst,) + ref_off[name] for name, dst in packed), o


PACK_SEGMENTS, N_IN = _pack_segments()


def _pack_kernel(w_ref, o_ref):
    for dst, src, n in PACK_SEGMENTS:
        o_ref[dst:dst + n, :] = w_ref[src:src + n, :].astype(BF16)
    used = OFF_DT + 2 * SSM_HEADS
    o_ref[used:, :] = jnp.zeros((N_PACK - used, o_ref.shape[1]), BF16)


def _pack_w_in(w_in_t, tl=256):
    L, n_in, d = w_in_t.shape
    assert n_in == N_IN
    return pl.pallas_call(
        _pack_kernel,
        out_shape=jax.ShapeDtypeStruct((L, N_PACK, d), BF16),
        grid=(L, d // tl),
        in_specs=[pl.BlockSpec((None, n_in, tl), lambda l, i: (l, 0, i))],
        out_specs=pl.BlockSpec((None, N_PACK, tl), lambda l, i: (l, 0, i)),
        compiler_params=_cparams(("arbitrary", "arbitrary")),
        name="pack_w_in",
    )(w_in_t)


def _rope_tables(n_tokens):
    rows = n_tokens // GRID_W
    row = np.repeat(np.arange(rows, dtype=np.float32), GRID_W)
    col = np.tile(np.arange(GRID_W, dtype=np.float32), rows)
    inv = np.float32(ROPE_THETA) ** (-np.arange(ROPE_PAIRS_PER_AXIS, dtype=np.float32) / ROPE_PAIRS_PER_AXIS)
    ang = np.concatenate([row[:, None] * inv, col[:, None] * inv], axis=-1).astype(np.float32)
    cos, sin = np.cos(ang), np.sin(ang)
    cos_f = np.stack([cos, cos], axis=-1).reshape(n_tokens, HEAD_DIM)
    sin_f = np.stack([-sin, sin], axis=-1).reshape(n_tokens, HEAD_DIM)
    return jnp.asarray(cos_f, F32), jnp.asarray(sin_f, F32)


def _pad_lanes(v, n):
    return jnp.concatenate([v, jnp.zeros((n - v.shape[0],), v.dtype)])[None, :]


def kernel(x, c, ctx, c_ctx, w_mod, b_mod, w_in, sgu_w, sgu_b, sgu_norm_g, conv_w, conv_b, dt_bias,
           a_log, d_skip, ssm_norm_g, q_norm_g, k_norm_g, w_branch, w_out, b_out, ln_g, ln_b):
    batch, seq, _ = x.shape
    n_ctx = ctx.shape[1]
    assert n_ctx == N_CTX_CHUNKS * CHUNK and seq % 1024 == 0

    cos_l, sin_l = _rope_tables(seq)
    cos_c = jnp.ones((n_ctx, HEAD_DIM), F32)
    sin_c = jnp.zeros((n_ctx, HEAD_DIM), F32)

    c_rows = jnp.concatenate([c, c_ctx[None, :], jnp.zeros((8 - batch - 1, D_MODEL), F32)], axis=0)
    mod_all = _modulation(c_rows, w_mod, b_mod[:, None, :])
    ctx_row = batch

    hid = jnp.arange(CHUNK)[:, None]
    ch = jnp.arange(D_MODEL)[None, :] // SSM_HEAD_DIM
    e_mats = jnp.stack([(hid == ch + d * SSM_HEADS).astype(BF16) for d in range(2)])

    lat_per_tile = 1024
    tiles_per_batch = seq // lat_per_tile
    xl = x.reshape(batch * seq, D_MODEL)
    xc = ctx.reshape(batch * n_ctx, D_MODEL)

    w_pack_all = _pack_w_in(jnp.swapaxes(w_in, 1, 2))
    shift_mat = _conv_shift_matrix()

    for l in range(DEPTH):
        ctx_out = l < DEPTH - 1
        mod = mod_all[l][:, None, :]
        wb = w_branch[l].astype(BF16)
        wo = w_out[l].astype(BF16)
        q_g, k_g = q_norm_g[l][None, :], k_norm_g[l][None, :]
        sgu_wb = sgu_w[l].astype(BF16)
        sgu_bfull = jnp.repeat(sgu_b[l].T, A_GROUP_DIM, axis=1)
        sgu_g = sgu_norm_g[l][None, :]
        dtb = _pad_lanes(dt_bias[l].reshape(-1), CHUNK)
        alog = _pad_lanes(a_log[l].reshape(-1), CHUNK)

        lat_row = lambda i: i // tiles_per_batch
        p_lat, dt_lat = _inproj(xl, mod, w_pack_all, l, lat_per_tile, lat_row)
        p_ctx, dt_ctx = _inproj(xc, mod, w_pack_all, l, n_ctx, lambda i: ctx_row)

        xconv_lat = _conv(p_lat, shift_mat, conv_w[l], conv_b[l][None, :], seq, 512)
        xconv_ctx = _conv(p_ctx, shift_mat, conv_w[l], conv_b[l][None, :], n_ctx, n_ctx)
        dskip_x = jnp.repeat(d_skip[l], SSM_HEAD_DIM, axis=1)[:, None, :]
        ssd = _ssd(xconv_ctx, xconv_lat, dt_ctx, dt_lat, dtb, alog, dskip_x, e_mats, batch)

        k_ctx, vt_ctx = _kvprep(p_ctx, cos_c, sin_c, k_g, batch, n_ctx)
        k_lat, vt_lat = _kvprep(p_lat, cos_l, sin_l, k_g, batch, 512)
        o_lat = _attention(p_lat, cos_l, sin_l, q_g, k_ctx, vt_ctx, k_lat, vt_lat, batch, 512, 1024)

        sgu = (sgu_wb, sgu_bfull, sgu_g)
        vecs = (b_out[l][None, :], ssm_norm_g[l][None, :], ln_g[l][None, :], ln_b[l][None, :])
        merge_tm = 512
        merge_tiles_per_batch = seq // merge_tm
        xl_new = _merge(*sgu, ssd[0][1], ssd[1][1], p_lat, o_lat, xl, mod, wb, wo, *vecs,
                        merge_tm, lambda i: i // merge_tiles_per_batch)
        if ctx_out:
            o_ctx = _attention(p_ctx, cos_c, sin_c, q_g, k_ctx, vt_ctx, None, None, batch, n_ctx, 0)
            xc = _merge(*sgu, ssd[0][0], ssd[1][0], p_ctx, o_ctx, xc, mod, wb, wo, *vecs,
                        n_ctx, lambda i: ctx_row)
        xl = xl_new

    return xl.reshape(batch, seq, D_MODEL)
```

```python
import functools
import math

import jax
import jax.numpy as jnp
import numpy as np
from jax import lax
from jax.experimental import pallas as pl
from jax.experimental.pallas import tpu as pltpu

D_MODEL = 1024
DEPTH = 2
GRID_W = 64
CHUNK = 128
A_GROUPS = 4
A_GROUP_DIM = D_MODEL // A_GROUPS
SSM_HEAD_DIM = 64
SSM_HEADS = 16
SSM_GROUPS = 2
SSM_STATE = 128
D_CONV = 5
CONV_DIM = D_MODEL + 2 * SSM_GROUPS * SSM_STATE
HEAD_DIM = 128
N_Q_HEADS = 8
N_KV_HEADS = 2
GQA_REP = N_Q_HEADS // N_KV_HEADS
KV_WIDTH = N_KV_HEADS * HEAD_DIM
ROPE_THETA = 10000.0
ROPE_PAIRS_PER_AXIS = HEAD_DIM // 4
DEEPNORM_ALPHA = (2 * DEPTH) ** 0.25
LN_EPS = 1e-6
RMS_EPS = 1e-6

OFF_U, OFF_V, OFF_GA, OFF_Z, OFF_Q, OFF_GC, OFF_GM = 0, 1024, 2048, 3072, 4096, 5120, 6144
OFF_XBC = 9216
OFF_K = OFF_XBC + CONV_DIM
OFF_VV = OFF_K + KV_WIDTH
OFF_DT = OFF_VV + KV_WIDTH
DT_PAD = 256
N_PACK = OFF_DT + DT_PAD
PROJ_TN = 1280

BF16_SUBLANES = 16
VT_ROWS = HEAD_DIM + BF16_SUBLANES

VMEM_LIMIT = 56 * 1024 * 1024
F32 = jnp.float32
BF16 = jnp.bfloat16
NEG_BIG = -1e30


def _cparams(sem):
    return pltpu.CompilerParams(dimension_semantics=sem, vmem_limit_bytes=VMEM_LIMIT)


def _sigmoid(t):
    return 0.5 * jnp.tanh(0.5 * t) + 0.5


def _silu(t):
    h = 0.5 * t
    return h * jnp.tanh(h) + h


def _split_dot(x, e, terms=3):
    acc = None
    r = x
    for t in range(terms):
        xt = r.astype(BF16)
        if t + 1 < terms:
            r = r - xt.astype(F32)
        part = jnp.dot(xt, e, preferred_element_type=F32)
        acc = part if acc is None else acc + part
    return acc


def _split_dot_rhs(e, x):
    x1 = x.astype(BF16)
    r1 = x - x1.astype(F32)
    x2 = r1.astype(BF16)
    r2 = r1 - x2.astype(F32)
    x3 = r2.astype(BF16)
    acc = jnp.dot(e, x1, preferred_element_type=F32)
    acc = acc + jnp.dot(e, x2, preferred_element_type=F32)
    return acc + jnp.dot(e, x3, preferred_element_type=F32)


def _mod_kernel(c_ref, w_ref, b_ref, o_ref):
    c = c_ref[...]
    act = _silu(c)
    o_ref[...] = jnp.dot(act, w_ref[...], preferred_element_type=F32,
                         precision=lax.Precision.HIGHEST) + b_ref[...]


def _modulation(c_rows, w_mod, b_mod):
    L = w_mod.shape[0]
    tn = 512
    return pl.pallas_call(
        _mod_kernel,
        out_shape=jax.ShapeDtypeStruct((L, 8, 3 * D_MODEL), F32),
        grid=(L, 3 * D_MODEL // tn),
        in_specs=[pl.BlockSpec((8, D_MODEL), lambda l, j: (0, 0)),
                  pl.BlockSpec((None, D_MODEL, tn), lambda l, j: (l, 0, j)),
                  pl.BlockSpec((None, 1, tn), lambda l, j: (l, 0, j))],
        out_specs=pl.BlockSpec((None, 8, tn), lambda l, j: (l, 0, j)),
        compiler_params=_cparams(("arbitrary", "arbitrary")),
        name="adaln_modulation",
    )(c_rows, w_mod, b_mod)


def _inproj_kernel(x_ref, mod_ref, w_ref, o_ref, dt_ref, h_ref):
    j = pl.program_id(1)

    @pl.when(j == 0)
    def _():
        x = x_ref[...]
        mu = jnp.mean(x, axis=-1, keepdims=True)
        xc = x - mu
        var = jnp.mean(xc * xc, axis=-1, keepdims=True)
        shift = mod_ref[:, 0:D_MODEL]
        scale = mod_ref[:, D_MODEL:2 * D_MODEL]
        h = xc * lax.rsqrt(var + LN_EPS) * (1.0 + scale) + shift
        h_ref[...] = h.astype(BF16)

    acc = lax.dot_general(h_ref[...], w_ref[...], (((1,), (1,)), ((), ())), preferred_element_type=F32)
    o_ref[...] = acc.astype(o_ref.dtype)

    @pl.when(j == pl.num_programs(1) - 1)
    def _():
        dt_ref[...] = acc[:, PROJ_TN - DT_PAD:]


def _inproj(x2d, mod, w_pack, layer, tm, mod_row):
    T = x2d.shape[0]
    assert OFF_DT + DT_PAD == N_PACK and N_PACK % PROJ_TN == 0
    return pl.pallas_call(
        _inproj_kernel,
        out_shape=(jax.ShapeDtypeStruct((T, N_PACK), BF16),
                   jax.ShapeDtypeStruct((T, DT_PAD), F32)),
        grid=(T // tm, N_PACK // PROJ_TN),
        in_specs=[pl.BlockSpec((tm, D_MODEL), lambda i, j: (i, 0)),
                  pl.BlockSpec((None, 1, 3 * D_MODEL), lambda i, j: (mod_row(i), 0, 0)),
                  pl.BlockSpec((None, PROJ_TN, D_MODEL), lambda i, j: (layer, j, 0))],
        out_specs=(pl.BlockSpec((tm, PROJ_TN), lambda i, j: (i, j)),
                   pl.BlockSpec((tm, DT_PAD), lambda i, j: (i, 0))),
        scratch_shapes=[pltpu.VMEM((tm, D_MODEL), BF16)],
        compiler_params=_cparams(("arbitrary", "arbitrary")),
        name="ln_mod_inproj",
    )(x2d, mod, w_pack)


def _sgu_kernel(u_ref, v_ref, ga_ref, w_ref, b_ref, g_ref, o_ref, *, n_chunks):
    for c in range(n_chunks):
        rows = slice(c * CHUNK, (c + 1) * CHUNK)
        for g in range(A_GROUPS):
            cols = slice(g * A_GROUP_DIM, (g + 1) * A_GROUP_DIM)
            v = v_ref[rows, cols].astype(F32)
            mu = jnp.mean(v, axis=-1, keepdims=True)
            vc = v - mu
            var = jnp.mean(vc * vc, axis=-1, keepdims=True)
            vn = vc * lax.rsqrt(var + LN_EPS) * g_ref[:, cols]
            mixed = jnp.dot(w_ref[g], vn.astype(BF16), preferred_element_type=F32)
            mixed = mixed + b_ref[:, cols]
            y = u_ref[rows, cols].astype(F32) * mixed * _silu(ga_ref[rows, cols].astype(F32))
            o_ref[rows, cols] = y.astype(o_ref.dtype)


CONV_HALO = 64
CONV_WINDOW = CHUNK + 2 * CONV_HALO


def _conv_shift_matrix():
    pad = (D_CONV - 1) // 2
    m = np.zeros((D_CONV * CHUNK, CONV_WINDOW), np.float32)
    for j in range(D_CONV):
        t = np.arange(CHUNK)
        m[j * CHUNK + t, t + CONV_HALO + j - pad] = 1.0
    return jnp.asarray(m, BF16)


def _conv_kernel(prev_ref, cur_ref, next_ref, shift_ref, w_ref, b_ref, o_ref, ext_ref, *, tiles_per_seq, tc):
    i = pl.program_id(0)
    pos = i % tiles_per_seq
    zeros = jnp.zeros((CONV_HALO, CONV_DIM), BF16)
    ext_ref[0:CONV_HALO, :] = jnp.where(pos == 0, zeros, prev_ref[...])
    ext_ref[CONV_HALO:CONV_HALO + tc, :] = cur_ref[...]
    ext_ref[CONV_HALO + tc:, :] = jnp.where(pos == tiles_per_seq - 1, zeros, next_ref[...])
    for k in range(tc // CHUNK):
        window = ext_ref[k * CHUNK:k * CHUNK + CONV_WINDOW, :]
        taps = jnp.dot(shift_ref[...], window, preferred_element_type=F32)
        acc = b_ref[...] + taps[0:CHUNK, :] * w_ref[0:1, :]
        for j in range(1, D_CONV):
            acc = acc + taps[j * CHUNK:(j + 1) * CHUNK, :] * w_ref[j:j + 1, :]
        o_ref[k * CHUNK:(k + 1) * CHUNK, :] = _silu(acc).astype(o_ref.dtype)


def _conv(p, shift_mat, conv_w, conv_b, seq_len, tc):
    T = p.shape[0]
    n_tiles = T // tc
    hb = tc // CONV_HALO
    n_hblocks = T // CONV_HALO
    colb = OFF_XBC // CONV_DIM
    return pl.pallas_call(
        functools.partial(_conv_kernel, tiles_per_seq=seq_len // tc, tc=tc),
        out_shape=jax.ShapeDtypeStruct((T, CONV_DIM), BF16),
        grid=(n_tiles,),
        in_specs=[pl.BlockSpec((CONV_HALO, CONV_DIM), lambda i: (jnp.maximum(i * hb - 1, 0), colb)),
                  pl.BlockSpec((tc, CONV_DIM), lambda i: (i, colb)),
                  pl.BlockSpec((CONV_HALO, CONV_DIM),
                               lambda i: (jnp.minimum((i + 1) * hb, n_hblocks - 1), colb)),
                  pl.BlockSpec((D_CONV * CHUNK, CONV_WINDOW), lambda i: (0, 0)),
                  pl.BlockSpec((D_CONV, CONV_DIM), lambda i: (0, 0)),
                  pl.BlockSpec((1, CONV_DIM), lambda i: (0, 0))],
        out_specs=pl.BlockSpec((tc, CONV_DIM), lambda i: (i, 0)),
        scratch_shapes=[pltpu.VMEM((tc + 2 * CONV_HALO, CONV_DIM), BF16)],
        compiler_params=_cparams(("arbitrary",)),
        name="dwconv_silu",
    )(p, p, p, shift_mat, conv_w, conv_b)


N_CTX_CHUNKS = 2


SSD_BLOCK = 512


def _ssd_scan(dt_raw, dtb, alog, d):
    reverse = d == 1
    dt = jax.nn.softplus(dt_raw + dtb)
    a = dt * (-jnp.exp(alog) * LOG2E)

    row = lax.broadcasted_iota(jnp.int32, (CHUNK, CHUNK), 0)
    col = lax.broadcasted_iota(jnp.int32, (CHUNK, CHUNK), 1)
    mask = (row <= col) if reverse else (row >= col)
    tri = mask.astype(BF16)
    tri_t = ((row >= col) if reverse else (row <= col)).astype(BF16)

    a_cs = _split_dot_rhs(tri, a)
    a_cs_t = _split_dot(a.T, tri_t)
    return dt, a_cs, a_cs_t, mask


def _ssd_group(g, scan, x_cols, dskip_ref, e_ref, ht_ref, y_ref, d):
    dt, a_cs, a_cs_t, mask = scan
    end = 0 if d == 1 else CHUNK - 1
    lane = lax.broadcasted_iota(jnp.int32, (CHUNK, 2 * SSM_HEAD_DIM), 1)
    left = lane < SSM_HEAD_DIM
    gw = D_MODEL // SSM_GROUPS
    heads_per_group = SSM_HEADS // SSM_GROUPS
    heads_per_block = SSD_BLOCK // SSM_HEAD_DIM
    b_g = x_cols(D_MODEL + g * SSM_STATE, D_MODEL + (g + 1) * SSM_STATE)
    c_g = x_cols(D_MODEL + (SSM_GROUPS + g) * SSM_STATE, D_MODEL + (SSM_GROUPS + g + 1) * SSM_STATE)
    cb = lax.dot_general(c_g, b_g, (((1,), (1,)), ((), ())), preferred_element_type=F32)
    b_t = b_g.astype(F32).T.astype(BF16)
    for blk in range(gw // SSD_BLOCK):
        c0 = g * gw + blk * SSD_BLOCK
        cols = slice(c0, c0 + SSD_BLOCK)
        gcols = slice(blk * SSD_BLOCK, (blk + 1) * SSD_BLOCK)
        e_blk = e_ref[:, cols]
        a_cs_x = _split_dot(a_cs, e_blk, terms=2)
        dt_x = _split_dot(dt, e_blk, terms=1)
        a_tot_x = a_cs_x[end:end + 1, :]
        xs = x_cols(c0, c0 + SSD_BLOCK).astype(F32)
        xdt = xs * dt_x
        xdt_bf = xdt.astype(BF16)
        xw = (xdt * jnp.exp2(a_tot_x - a_cs_x)).astype(BF16)
        h_in = ht_ref[g, :, gcols]
        y_blk = jnp.dot(c_g, h_in.astype(BF16), preferred_element_type=F32) * jnp.exp2(a_cs_x)
        ht_ref[g, :, gcols] = h_in * jnp.exp2(a_tot_x) + jnp.dot(b_t, xw, preferred_element_type=F32)
        y_pairs = []
        for j in range(heads_per_block // 2):
            ms = []
            for k in range(2):
                hh = d * SSM_HEADS + g * heads_per_group + blk * heads_per_block + 2 * j + k
                seg = a_cs[:, hh:hh + 1] - a_cs_t[hh:hh + 1, :]
                dec = jnp.exp2(jnp.where(mask, seg, NEG_BIG))
                ms.append((cb * dec).astype(BF16))
            lhs = jnp.concatenate(ms, axis=1)
            xp = xdt_bf[:, 2 * j * SSM_HEAD_DIM:(2 * j + 2) * SSM_HEAD_DIM]
            zero = jnp.zeros_like(xp)
            rhs = jnp.concatenate([jnp.where(left, xp, zero), jnp.where(left, zero, xp)], axis=0)
            y_pairs.append(jnp.dot(lhs, rhs, preferred_element_type=F32))
        y_blk = y_blk + jnp.concatenate(y_pairs, axis=1) + dskip_ref[:, cols] * xs
        y_ref[:, cols] = y_blk.astype(y_ref.dtype)


def _ssd_kernel(xcf_ref, xlf_ref, dtcf_ref, dtlf_ref, xcr_ref, xlr_ref, dtcr_ref, dtlr_ref,
                dtb_ref, alog_ref, dskip_ref, e_ref,
                ycf_ref, ylf_ref, ycr_ref, ylr_ref, ht_ref, y_ref):
    s = pl.program_id(0)
    is_ctx = s < N_CTX_CHUNKS
    batch = xcf_ref.shape[0]

    @pl.when(s == 0)
    def _():
        ht_ref[...] = jnp.zeros_like(ht_ref)

    sides = ((xcf_ref, xlf_ref, dtcf_ref, dtlf_ref, ycf_ref, ylf_ref),
             (xcr_ref, xlr_ref, dtcr_ref, dtlr_ref, ycr_ref, ylr_ref))
    chains = []
    for b in range(batch):
        for d, (xc_ref, xl_ref, dtc_ref, dtl_ref, _, _) in enumerate(sides):
            x_cols = lambda c0, c1, b=b, xc_ref=xc_ref, xl_ref=xl_ref: jnp.where(
                is_ctx, xc_ref[b, :, c0:c1], xl_ref[b, :, c0:c1])
            dt_raw = jnp.where(is_ctx, dtc_ref[b, :, 0:CHUNK], dtl_ref[b, :, 0:CHUNK])
            chains.append((b, d, x_cols, _ssd_scan(dt_raw, dtb_ref[...], alog_ref[...], d)))
    for g in range(SSM_GROUPS):
        for b, d, x_cols, scan in chains:
            _ssd_group(g, scan, x_cols, dskip_ref.at[d], e_ref.at[d], ht_ref.at[b, d], y_ref.at[b, d], d)

    @pl.when(is_ctx)
    def _():
        for d in range(2):
            sides[d][4][...] = y_ref[:, d]

    @pl.when(jnp.logical_not(is_ctx))
    def _():
        for d in range(2):
            sides[d][5][...] = y_ref[:, d]


def _ssd(xconv_ctx, xconv_lat, dt_ctx, dt_lat, dtb, alog, dskip_x, e_mats, batch):
    n_lat = xconv_lat.shape[0] // batch // CHUNK
    n_steps = N_CTX_CHUNKS + n_lat
    per_batch = lambda t: t.reshape(batch, t.shape[0] // batch, t.shape[1])

    cidx = (lambda s: jnp.minimum(s, N_CTX_CHUNKS - 1),
            lambda s: jnp.maximum(N_CTX_CHUNKS - 1 - s, 0))
    lidx = (lambda s: jnp.maximum(s - N_CTX_CHUNKS, 0),
            lambda s: jnp.minimum(n_steps - 1 - s, n_lat - 1))

    def specs(width, d):
        return [pl.BlockSpec((batch, CHUNK, width), lambda s, d=d: (0, cidx[d](s), 0)),
                pl.BlockSpec((batch, CHUNK, width), lambda s, d=d: (0, lidx[d](s), 0))]

    const = lambda shape: pl.BlockSpec(shape, lambda s: (0,) * len(shape))
    in_specs, out_specs, args = [], [], []
    for d in range(2):
        in_specs += specs(CONV_DIM, d) + specs(DT_PAD, d)
        args += [per_batch(xconv_ctx), per_batch(xconv_lat), per_batch(dt_ctx), per_batch(dt_lat)]
        out_specs += specs(D_MODEL, d)
    in_specs += [const((1, CHUNK)), const((1, CHUNK)), const((2, 1, D_MODEL)), const((2, CHUNK, D_MODEL))]
    y_shapes = (jax.ShapeDtypeStruct((batch, xconv_ctx.shape[0] // batch, D_MODEL), BF16),
                jax.ShapeDtypeStruct((batch, xconv_lat.shape[0] // batch, D_MODEL), BF16))
    ys = pl.pallas_call(
        _ssd_kernel,
        out_shape=y_shapes + y_shapes,
        grid=(n_steps,),
        in_specs=in_specs,
        out_specs=tuple(out_specs),
        scratch_shapes=[pltpu.VMEM((batch, 2, SSM_GROUPS, SSM_STATE, D_MODEL // SSM_GROUPS), F32),
                        pltpu.VMEM((batch, 2, CHUNK, D_MODEL), BF16)],
        compiler_params=_cparams(("arbitrary",)),
        name="ssd_scan_bidir",
    )(*args, dtb, alog, dskip_x, e_mats)
    ycf, ylf, ycr, ylr = (y.reshape(-1, D_MODEL) for y in ys)
    return (ycf, ylf), (ycr, ylr)


def _norm_rope(t, g, cos, sin):
    tn = t * lax.rsqrt(jnp.mean(t * t, axis=-1, keepdims=True) + RMS_EPS) * g
    lane = lax.broadcasted_iota(jnp.int32, tn.shape, 1)
    partner = jnp.where(lane % 2 == 0, pltpu.roll(tn, HEAD_DIM - 1, axis=1), pltpu.roll(tn, 1, axis=1))
    return tn * cos + partner * sin


def _kvprep_kernel(k_ref, v_ref, cos_ref, sin_ref, g_ref, ko_ref, vto_ref):
    for h in range(N_KV_HEADS):
        cols = slice(h * HEAD_DIM, (h + 1) * HEAD_DIM)
        kh = _norm_rope(k_ref[:, cols].astype(F32), g_ref[...], cos_ref[...], sin_ref[...])
        ko_ref[h] = kh.astype(BF16)
        vto_ref[h, 0:HEAD_DIM, :] = v_ref[:, cols].astype(F32).T.astype(BF16)
        vto_ref[h, HEAD_DIM:, :] = jnp.ones((VT_ROWS - HEAD_DIM, k_ref.shape[0]), BF16)


def _kvprep(p, cos, sin, k_g, batch, tm):
    T = p.shape[0]
    S = T // batch
    nt = S // tm
    return pl.pallas_call(
        _kvprep_kernel,
        out_shape=(jax.ShapeDtypeStruct((batch, N_KV_HEADS, S, HEAD_DIM), BF16),
                   jax.ShapeDtypeStruct((batch, N_KV_HEADS, VT_ROWS, S), BF16)),
        grid=(batch, nt),
        in_specs=[pl.BlockSpec((tm, KV_WIDTH), lambda b, i: (b * nt + i, OFF_K // KV_WIDTH)),
                  pl.BlockSpec((tm, KV_WIDTH), lambda b, i: (b * nt + i, OFF_VV // KV_WIDTH)),
                  pl.BlockSpec((tm, HEAD_DIM), lambda b, i: (i, 0)),
                  pl.BlockSpec((tm, HEAD_DIM), lambda b, i: (i, 0)),
                  pl.BlockSpec((1, HEAD_DIM), lambda b, i: (0, 0))],
        out_specs=(pl.BlockSpec((None, N_KV_HEADS, tm, HEAD_DIM), lambda b, i: (b, 0, i, 0)),
                   pl.BlockSpec((None, N_KV_HEADS, VT_ROWS, tm), lambda b, i: (b, 0, 0, i))),
        compiler_params=_cparams(("arbitrary", "arbitrary")),
        name="kv_prep",
    )(p, p, cos, sin, k_g)


LOG2E = 1.4426950408889634
ATTN_LANE_BLOCK = 256


def _attn_kernel(*refs, tq, tk, n_lat_chunks):
    if n_lat_chunks:
        (q_ref, gc_ref, cos_ref, sin_ref, g_ref, kc_ref, vtc_ref, kl_ref, vtl_ref, o_ref,
         qt_ref, acc_ref, s0_ref, s1_ref) = refs
    else:
        q_ref, gc_ref, cos_ref, sin_ref, g_ref, kc_ref, vtc_ref, o_ref, qt_ref, acc_ref = refs
    scale = HEAD_DIM ** -0.5 * LOG2E

    lb = min(tq, ATTN_LANE_BLOCK)
    blocks = [(h, r0, slice(h * tq + r0, h * tq + r0 + lb)) for h in range(GQA_REP) for r0 in range(0, tq, lb)]

    def offset(j):
        return j * tk if isinstance(j, int) else pl.multiple_of(j * tk, tk)

    def k_chunk(j):
        return kl_ref[pl.ds(offset(j), tk), :]

    def vt_chunk(j):
        return vtl_ref[:, pl.ds(offset(j), tk)]

    def finish(acc, h, r0):
        cols = slice(h * HEAD_DIM, (h + 1) * HEAD_DIM)
        rows = slice(r0, r0 + lb)
        oh = (acc[0:HEAD_DIM, :] * (1.0 / acc[HEAD_DIM:HEAD_DIM + 1, :])).T
        o_ref[rows, cols] = (oh * _silu(gc_ref[rows, cols].astype(F32))).astype(o_ref.dtype)

    m, cm0 = [], []
    for h, r0, lanes in blocks:
        cols = slice(h * HEAD_DIM, (h + 1) * HEAD_DIM)
        rows = slice(r0, r0 + lb)
        qh = _norm_rope(q_ref[rows, cols].astype(F32), g_ref[...], cos_ref[rows, :], sin_ref[rows, :]) * scale
        qt_ref[:, lanes] = qh.T.astype(BF16)
        st = jnp.dot(kc_ref[...], qt_ref[:, lanes], preferred_element_type=F32)
        mb = jnp.max(st, axis=0, keepdims=True)
        p = jnp.exp2(st - mb)
        acc = jnp.dot(vtc_ref[...], p.astype(BF16), preferred_element_type=F32)
        if n_lat_chunks:
            acc_ref[:, lanes] = acc
        else:
            finish(acc, h, r0)
        m.append(mb)
        if n_lat_chunks:
            st = jnp.dot(k_chunk(0), qt_ref[:, lanes], preferred_element_type=F32)
            s0_ref[:, lanes] = st
            cm0.append(jnp.max(st, axis=0, keepdims=True))
    m, cm0 = tuple(m), tuple(cm0)

    if n_lat_chunks:
        def stage(s_cur, s_next, j_cur, j_next, m, cm_cur):
            m_out, cm_next = [], []
            for bi, (h, r0, lanes) in enumerate(blocks):
                if j_next is not None:
                    st = jnp.dot(k_chunk(j_next), qt_ref[:, lanes], preferred_element_type=F32)
                    s_next[:, lanes] = st
                    cm_next.append(jnp.max(st, axis=0, keepdims=True))
                m_new = jnp.maximum(m[bi], cm_cur[bi])
                alpha = jnp.exp2(m[bi] - m_new)
                p = jnp.exp2(s_cur[:, lanes] - m_new)
                acc = alpha * acc_ref[:, lanes] + jnp.dot(vt_chunk(j_cur), p.astype(BF16),
                                                          preferred_element_type=F32)
                if j_next is not None:
                    acc_ref[:, lanes] = acc
                else:
                    finish(acc, h, r0)
                m_out.append(m_new)
            return tuple(m_out), tuple(cm_next)

        def pair(jj, carry):
            m, cm0 = carry
            j = 2 * jj
            m, cm1 = stage(s0_ref, s1_ref, j, j + 1, m, cm0)
            m, cm0 = stage(s1_ref, s0_ref, j + 1, j + 2, m, cm1)
            return m, cm0

        m, cm0 = lax.fori_loop(0, n_lat_chunks // 2 - 1, pair, (m, cm0))
        j = n_lat_chunks - 2
        m, cm1 = stage(s0_ref, s1_ref, j, j + 1, m, cm0)
        m, _ = stage(s1_ref, s0_ref, j + 1, None, m, cm1)


def _attention(p_q, cos, sin, q_g, k_ctx, vt_ctx, k_lat, vt_lat, batch, tq, tk):
    T = p_q.shape[0]
    S = T // batch
    nq = S // tq
    n_ctx = k_ctx.shape[2]
    n_lat_chunks = 0 if k_lat is None else k_lat.shape[2] // tk
    assert n_lat_chunks % 2 == 0
    gw = GQA_REP * HEAD_DIM
    in_specs = [pl.BlockSpec((tq, gw), lambda b, g, i: (b * nq + i, OFF_Q // gw + g)),
                pl.BlockSpec((tq, gw), lambda b, g, i: (b * nq + i, OFF_GC // gw + g)),
                pl.BlockSpec((tq, HEAD_DIM), lambda b, g, i: (i, 0)),
                pl.BlockSpec((tq, HEAD_DIM), lambda b, g, i: (i, 0)),
                pl.BlockSpec((1, HEAD_DIM), lambda b, g, i: (0, 0)),
                pl.BlockSpec((None, None, n_ctx, HEAD_DIM), lambda b, g, i: (b, g, 0, 0)),
                pl.BlockSpec((None, None, VT_ROWS, n_ctx), lambda b, g, i: (b, g, 0, 0))]
    args = [p_q, p_q, cos, sin, q_g, k_ctx, vt_ctx]
    scratch = [pltpu.VMEM((HEAD_DIM, GQA_REP * tq), BF16), pltpu.VMEM((VT_ROWS, GQA_REP * tq), F32)]
    if n_lat_chunks:
        n_lat = k_lat.shape[2]
        in_specs += [pl.BlockSpec((None, None, n_lat, HEAD_DIM), lambda b, g, i: (b, g, 0, 0)),
                     pl.BlockSpec((None, None, VT_ROWS, n_lat), lambda b, g, i: (b, g, 0, 0))]
        args += [k_lat, vt_lat]
        scratch += [pltpu.VMEM((tk, GQA_REP * tq), F32), pltpu.VMEM((tk, GQA_REP * tq), F32)]
    return pl.pallas_call(
        functools.partial(_attn_kernel, tq=tq, tk=tk, n_lat_chunks=n_lat_chunks),
        out_shape=jax.ShapeDtypeStruct((T, D_MODEL), BF16),
        grid=(batch, N_KV_HEADS, nq),
        in_specs=in_specs,
        out_specs=pl.BlockSpec((tq, gw), lambda b, g, i: (b * nq + i, g)),
        scratch_shapes=scratch,
        compiler_params=_cparams(("arbitrary", "arbitrary", "arbitrary")),
        name="gqa_flash_lat" if n_lat_chunks else "gqa_flash_ctx",
    )(*args)


MERGE_ROW_BLOCK = 256


def _merge_kernel(u_ref, v_ref, ga_ref, sw_ref, sbias_ref, sgain_ref, sf_ref, sb_ref, z_ref, yc_ref, gm_ref,
                  x_ref, mod_ref, wb_ref, wo_ref, bo_ref, sg_ref, lg_ref, lb_ref, out_ref, ya_ref):
    tm = u_ref.shape[0]
    rb = min(tm, MERGE_ROW_BLOCK)
    for r0 in range(0, tm, rb):
        rows = slice(r0, r0 + rb)
        _sgu_kernel(u_ref.at[rows], v_ref.at[rows], ga_ref.at[rows], sw_ref, sbias_ref, sgain_ref,
                    ya_ref.at[rows], n_chunks=rb // CHUNK)
        s = (sf_ref[rows, :].astype(F32) + sb_ref[rows, :].astype(F32)) * _silu(z_ref[rows, :].astype(F32))
        yb = s * lax.rsqrt(jnp.mean(s * s, axis=-1, keepdims=True) + RMS_EPS) * sg_ref[...]
        gate_of = lambda k: _sigmoid(gm_ref[rows, k * D_MODEL:(k + 1) * D_MODEL].astype(F32))
        merged = gate_of(0) * jnp.dot(ya_ref[rows, :], wb_ref[0], preferred_element_type=F32)
        merged = merged + gate_of(1) * jnp.dot(yb.astype(BF16), wb_ref[1], preferred_element_type=F32)
        merged = merged + gate_of(2) * jnp.dot(yc_ref[rows, :], wb_ref[2], preferred_element_type=F32)
        out = jnp.dot(merged.astype(BF16), wo_ref[...], preferred_element_type=F32) + bo_ref[...]
        gate = mod_ref[:, 2 * D_MODEL:3 * D_MODEL]
        r = DEEPNORM_ALPHA * x_ref[rows, :] + gate * out
        mu = jnp.mean(r, axis=-1, keepdims=True)
        rc = r - mu
        var = jnp.mean(rc * rc, axis=-1, keepdims=True)
        out_ref[rows, :] = rc * lax.rsqrt(var + LN_EPS) * lg_ref[...] + lb_ref[...]


def _merge(sgu_w, sgu_bfull, sgu_g, s_f, s_b, p, y_c, x2d, mod, w_branch, w_out, b_out, ssm_g, ln_g, ln_b,
           tm, mod_row):
    T = x2d.shape[0]
    row = lambda: pl.BlockSpec((tm, D_MODEL), lambda i: (i, 0))
    pcol = lambda off: pl.BlockSpec((tm, D_MODEL), lambda i: (i, off // D_MODEL))
    vec = lambda: pl.BlockSpec((1, D_MODEL), lambda i: (0, 0))
    once = pl.Buffered(1)
    return pl.pallas_call(
        _merge_kernel,
        out_shape=jax.ShapeDtypeStruct((T, D_MODEL), F32),
        grid=(T // tm,),
        in_specs=[pcol(OFF_U), pcol(OFF_V), pcol(OFF_GA),
                  pl.BlockSpec((A_GROUPS, CHUNK, CHUNK), lambda i: (0, 0, 0)),
                  pl.BlockSpec((CHUNK, D_MODEL), lambda i: (0, 0)),
                  vec(),
                  row(), row(), pcol(OFF_Z), row(),
                  pl.BlockSpec((tm, 3 * D_MODEL), lambda i: (i, OFF_GM // (3 * D_MODEL))),
                  row(),
                  pl.BlockSpec((None, 1, 3 * D_MODEL), lambda i: (mod_row(i), 0, 0)),
                  pl.BlockSpec((3, D_MODEL, D_MODEL), lambda i: (0, 0, 0), pipeline_mode=once),
                  pl.BlockSpec((D_MODEL, D_MODEL), lambda i: (0, 0), pipeline_mode=once),
                  vec(), vec(), vec(), vec()],
        out_specs=row(),
        scratch_shapes=[pltpu.VMEM((tm, D_MODEL), BF16)],
        compiler_params=_cparams(("arbitrary",)),
        name="sgu_merge_out_deepnorm",
    )(p, p, p, sgu_w, sgu_bfull, sgu_g, s_f, s_b, p, y_c, p, x2d, mod, w_branch, w_out, b_out, ssm_g, ln_g, ln_b)


def _pack_segments():
    ref_off, o = {}, 0
    for name, n in (("u", 1024), ("v", 1024), ("ga", 1024), ("z", 1024), ("xbc", CONV_DIM),
                    ("dt", 2 * SSM_HEADS), ("q", 1024), ("k", KV_WIDTH), ("vv", KV_WIDTH),
                    ("gc", 1024), ("gm", 3 * D_MODEL)):
        ref_off[name] = (o, n)
        o += n
    packed = (("u", OFF_U), ("v", OFF_V), ("ga", OFF_GA), ("z", OFF_Z), ("q", OFF_Q), ("gc", OFF_GC),
              ("gm", OFF_GM), ("xbc", OFF_XBC), ("k", OFF_K), ("vv", OFF_VV), ("dt", OFF_DT))
    return tuple((dst,) + ref_off[name] for name, dst in packed), o


PACK_SEGMENTS, N_IN = _pack_segments()


def _pack_kernel(w_ref, o_ref):
    for dst, src, n in PACK_SEGMENTS:
        o_ref[dst:dst + n, :] = w_ref[src:src + n, :].astype(BF16)
    used = OFF_DT + 2 * SSM_HEADS
    o_ref[used:, :] = jnp.zeros((N_PACK - used, o_ref.shape[1]), BF16)


def _pack_w_in(w_in_t, tl=256):
    L, n_in, d = w_in_t.shape
    assert n_in == N_IN
    return pl.pallas_call(
        _pack_kernel,
        out_shape=jax.ShapeDtypeStruct((L, N_PACK, d), BF16),
        grid=(L, d // tl),
        in_specs=[pl.BlockSpec((None, n_in, tl), lambda l, i: (l, 0, i))],
        out_specs=pl.BlockSpec((None, N_PACK, tl), lambda l, i: (l, 0, i)),
        compiler_params=_cparams(("arbitrary", "arbitrary")),
        name="pack_w_in",
    )(w_in_t)


def _rope_tables(n_tokens):
    rows = n_tokens // GRID_W
    row = np.repeat(np.arange(rows, dtype=np.float32), GRID_W)
    col = np.tile(np.arange(GRID_W, dtype=np.float32), rows)
    inv = np.float32(ROPE_THETA) ** (-np.arange(ROPE_PAIRS_PER_AXIS, dtype=np.float32) / ROPE_PAIRS_PER_AXIS)
    ang = np.concatenate([row[:, None] * inv, col[:, None] * inv], axis=-1).astype(np.float32)
    cos, sin = np.cos(ang), np.sin(ang)
    cos_f = np.stack([cos, cos], axis=-1).reshape(n_tokens, HEAD_DIM)
    sin_f = np.stack([-sin, sin], axis=-1).reshape(n_tokens, HEAD_DIM)
    return jnp.asarray(cos_f, F32), jnp.asarray(sin_f, F32)


def _pad_lanes(v, n):
    return jnp.concatenate([v, jnp.zeros((n - v.shape[0],), v.dtype)])[None, :]


def kernel(x, c, ctx, c_ctx, w_mod, b_mod, w_in, sgu_w, sgu_b, sgu_norm_g, conv_w, conv_b, dt_bias,
           a_log, d_skip, ssm_norm_g, q_norm_g, k_norm_g, w_branch, w_out, b_out, ln_g, ln_b):
    batch, seq, _ = x.shape
    n_ctx = ctx.shape[1]
    assert n_ctx == N_CTX_CHUNKS * CHUNK and seq % 1024 == 0

    cos_l, sin_l = _rope_tables(seq)
    cos_c = jnp.ones((n_ctx, HEAD_DIM), F32)
    sin_c = jnp.zeros((n_ctx, HEAD_DIM), F32)

    c_rows = jnp.concatenate([c, c_ctx[None, :], jnp.zeros((8 - batch - 1, D_MODEL), F32)], axis=0)
    mod_all = _modulation(c_rows, w_mod, b_mod[:, None, :])
    ctx_row = batch

    hid = jnp.arange(CHUNK)[:, None]
    ch = jnp.arange(D_MODEL)[None, :] // SSM_HEAD_DIM
    e_mats = jnp.stack([(hid == ch + d * SSM_HEADS).astype(BF16) for d in range(2)])

    lat_per_tile = 1024
    tiles_per_batch = seq // lat_per_tile
    xl = x.reshape(batch * seq, D_MODEL)
    xc = ctx.reshape(batch * n_ctx, D_MODEL)

    w_pack_all = _pack_w_in(jnp.swapaxes(w_in, 1, 2))
    shift_mat = _conv_shift_matrix()

    for l in range(DEPTH):
        ctx_out = l < DEPTH - 1
        mod = mod_all[l][:, None, :]
        wb = w_branch[l].astype(BF16)
        wo = w_out[l].astype(BF16)
        q_g, k_g = q_norm_g[l][None, :], k_norm_g[l][None, :]
        sgu_wb = sgu_w[l].astype(BF16)
        sgu_bfull = jnp.repeat(sgu_b[l].T, A_GROUP_DIM, axis=1)
        sgu_g = sgu_norm_g[l][None, :]
        dtb = _pad_lanes(dt_bias[l].reshape(-1), CHUNK)
        alog = _pad_lanes(a_log[l].reshape(-1), CHUNK)

        lat_row = lambda i: i // tiles_per_batch
        p_lat, dt_lat = _inproj(xl, mod, w_pack_all, l, lat_per_tile, lat_row)
        p_ctx, dt_ctx = _inproj(xc, mod, w_pack_all, l, n_ctx, lambda i: ctx_row)

        xconv_lat = _conv(p_lat, shift_mat, conv_w[l], conv_b[l][None, :], seq, 512)
        xconv_ctx = _conv(p_ctx, shift_mat, conv_w[l], conv_b[l][None, :], n_ctx, n_ctx)
        dskip_x = jnp.repeat(d_skip[l], SSM_HEAD_DIM, axis=1)[:, None, :]
        ssd = _ssd(xconv_ctx, xconv_lat, dt_ctx, dt_lat, dtb, alog, dskip_x, e_mats, batch)

        k_ctx, vt_ctx = _kvprep(p_ctx, cos_c, sin_c, k_g, batch, n_ctx)
        k_lat, vt_lat = _kvprep(p_lat, cos_l, sin_l, k_g, batch, 512)
        o_lat = _attention(p_lat, cos_l, sin_l, q_g, k_ctx, vt_ctx, k_lat, vt_lat, batch, 512, 1024)

        sgu = (sgu_wb, sgu_bfull, sgu_g)
        vecs = (b_out[l][None, :], ssm_norm_g[l][None, :], ln_g[l][None, :], ln_b[l][None, :])
        merge_tm = 512
        merge_tiles_per_batch = seq // merge_tm
        xl_new = _merge(*sgu, ssd[0][1], ssd[1][1], p_lat, o_lat, xl, mod, wb, wo, *vecs,
                        merge_tm, lambda i: i // merge_tiles_per_batch)
        if ctx_out:
            o_ctx = _attention(p_ctx, cos_c, sin_c, q_g, k_ctx, vt_ctx, None, None, batch, n_ctx, 0)
            xc = _merge(*sgu, ssd[0][0], ssd[1][0], p_ctx, o_ctx, xc, mod, wb, wo, *vecs,
                        n_ctx, lambda i: ctx_row)
        xl = xl_new

    return xl.reshape(batch, seq, D_MODEL)
```
